```python
import jax, jax.numpy as jnp
from jax import lax
import numpy as np

D_MODEL = 2048
BATCH = 8
SEQ = 2048
DEPTH = 2

GRID_W = 64
HEAD_DIM = 128
A_HEADS = 8
A_KV_HEADS = 2
B_HEADS = 4
B_Q_LORA = 512
B_KV_LORA = 256
B_NOPE = 128
B_ROPE = 64
B_V = 128
C_HEADS = 4
WIN_ROWS = 8
WIN_COLS = 16
D_MIX = A_HEADS * HEAD_DIM + B_HEADS * B_V + C_HEADS * HEAD_DIM
Q_BLOCK = 128
ROPE_THETA = 10000.0
NORM_EPS = 1e-6
IN_SPLITS = (
    A_HEADS * HEAD_DIM,
    A_KV_HEADS * HEAD_DIM,
    A_KV_HEADS * HEAD_DIM,
    A_HEADS * HEAD_DIM,
    B_Q_LORA,
    B_KV_LORA,
    B_ROPE,
    B_HEADS * B_V,
    C_HEADS * HEAD_DIM,
    C_HEADS * HEAD_DIM,
    C_HEADS * HEAD_DIM,
    C_HEADS * HEAD_DIM,
)
D_IN = sum(IN_SPLITS)

kernel_name = "hybrid_gqa_mla_natten_encoder"


def _rms_norm(x, g):
    xf = x.astype(jnp.float32)
    y = xf * lax.rsqrt(jnp.mean(xf * xf, axis=-1, keepdims=True) + NORM_EPS)
    return (y * g.astype(jnp.float32)).astype(x.dtype)


def _rope_half(x, pos):
    n = x.shape[-1]
    inv_freq = 1.0 / (ROPE_THETA ** (jnp.arange(0, n, 2, dtype=jnp.float32) / n))
    ang = pos.astype(jnp.float32)[:, None] * inv_freq[None, :]
    cos = jnp.cos(ang)[:, None, :]
    sin = jnp.sin(ang)[:, None, :]
    xf = x.astype(jnp.float32)
    x1, x2 = xf[..., : n // 2], xf[..., n // 2:]
    out = jnp.concatenate([x1 * cos - x2 * sin, x2 * cos + x1 * sin], axis=-1)
    return out.astype(x.dtype)


def _rope_axial(x, row, col):
    h = x.shape[-1] // 2
    return jnp.concatenate([_rope_half(x[..., :h], row), _rope_half(x[..., h:], col)], axis=-1)


def _dense_block_attention(q, k, v):
    B, S, H, dq = q.shape
    Hk = k.shape[2]
    G = H // Hk
    dv = v.shape[-1]
    nb = S // Q_BLOCK
    scale = dq ** -0.5
    qb = q.reshape(B, nb, Q_BLOCK, Hk, G, dq).transpose(1, 0, 2, 3, 4, 5)

    def one(qblk):
        s = jnp.einsum('bqkgd,bskd->bkgqs', qblk, k).astype(jnp.float32) * scale
        p = jax.nn.softmax(s, axis=-1).astype(v.dtype)
        return jnp.einsum('bkgqs,bskd->bqkgd', p, v)

    o = lax.map(one, qb)
    return o.transpose(1, 0, 2, 3, 4, 5).reshape(B, S, H, dv)


def _neighbourhood_attention(q, k, v, rpb):
    B, S, H, d = q.shape
    rows = S // GRID_W
    kr = min(WIN_ROWS, rows)
    nk = kr * WIN_COLS
    t = jnp.arange(S)
    r = t // GRID_W
    c = t % GRID_W
    r0 = jnp.clip(r - kr // 2, 0, rows - kr)
    c0 = jnp.clip(c - WIN_COLS // 2, 0, GRID_W - WIN_COLS)
    key_r = r0[:, None, None] + jnp.arange(kr)[None, :, None]
    key_c = c0[:, None, None] + jnp.arange(WIN_COLS)[None, None, :]
    idx = (key_r * GRID_W + key_c).reshape(S, nk)
    dr = key_r - r[:, None, None] + (WIN_ROWS - 1)
    dc = key_c - c[:, None, None] + (WIN_COLS - 1)
    bias = rpb[:, dr, dc].reshape(H, S, nk)
    nb = S // Q_BLOCK
    scale = d ** -0.5
    qb = q.reshape(B, nb, Q_BLOCK, H, d).transpose(1, 0, 2, 3, 4)
    idxb = idx.reshape(nb, Q_BLOCK, nk)
    biasb = bias.reshape(H, nb, Q_BLOCK, nk).transpose(1, 0, 2, 3)

    def one(args):
        qblk, iblk, bblk = args
        kg = jnp.take(k, iblk, axis=1)
        vg = jnp.take(v, iblk, axis=1)
        s = jnp.einsum('bqhd,bqnhd->bhqn', qblk, kg).astype(jnp.float32) * scale
        s = s + bblk[None].astype(jnp.float32)
        p = jax.nn.softmax(s, axis=-1).astype(v.dtype)
        return jnp.einsum('bhqn,bqnhd->bqhd', p, vg)

    o = lax.map(one, (qb, idxb, biasb))
    return o.transpose(1, 0, 2, 3, 4).reshape(B, S, H, d)


def _layer(x, g_pre, g_post, w_in, a_qn, a_kn, b_qn, b_kvn, w_uq, w_ukv, rpb, w_out, row, col):
    B, S, _ = x.shape
    h = _rms_norm(x, g_pre)
    proj = jnp.einsum('bsd,de->bse', h, w_in)
    split_points = [int(p) for p in np.cumsum(IN_SPLITS)[:-1]]
    (a_q, a_k, a_v, a_g, b_cq, b_ckv, b_kr, b_g,
     c_q, c_k, c_v, c_g) = jnp.split(proj, split_points, axis=-1)

    qa = _rms_norm(a_q.reshape(B, S, A_HEADS, HEAD_DIM), a_qn)
    ka = _rms_norm(a_k.reshape(B, S, A_KV_HEADS, HEAD_DIM), a_kn)
    va = a_v.reshape(B, S, A_KV_HEADS, HEAD_DIM)
    qa = _rope_axial(qa, row, col)
    ka = _rope_axial(ka, row, col)
    o_a = _dense_block_attention(qa, ka, va).reshape(B, S, A_HEADS * HEAD_DIM) * jax.nn.silu(a_g)

    cq = _rms_norm(b_cq, b_qn)
    qb = jnp.einsum('bsr,re->bse', cq, w_uq).reshape(B, S, B_HEADS, B_NOPE + B_ROPE)
    q_nope, q_pe = qb[..., :B_NOPE], qb[..., B_NOPE:]
    q_pe = _rope_axial(q_pe, row, col)
    ckv = _rms_norm(b_ckv, b_kvn)
    kv = jnp.einsum('bsr,re->bse', ckv, w_ukv).reshape(B, S, B_HEADS, B_NOPE + B_V)
    k_nope, vb = kv[..., :B_NOPE], kv[..., B_NOPE:]
    k_pe = _rope_axial(b_kr.reshape(B, S, 1, B_ROPE), row, col)
    k_pe = jnp.broadcast_to(k_pe, (B, S, B_HEADS, B_ROPE))
    q_full = jnp.concatenate([q_nope, q_pe], axis=-1)
    k_full = jnp.concatenate([k_nope, k_pe], axis=-1)
    o_b = _dense_block_attention(q_full, k_full, vb).reshape(B, S, B_HEADS * B_V) * jax.nn.silu(b_g)

    qc = c_q.reshape(B, S, C_HEADS, HEAD_DIM)
    kc = c_k.reshape(B, S, C_HEADS, HEAD_DIM)
    vc = c_v.reshape(B, S, C_HEADS, HEAD_DIM)
    o_c = _neighbourhood_attention(qc, kc, vc, rpb).reshape(B, S, C_HEADS * HEAD_DIM) * jax.nn.silu(c_g)

    y = jnp.einsum('bse,ed->bsd', jnp.concatenate([o_a, o_b, o_c], axis=-1), w_out)
    return x + _rms_norm(y, g_post)


def setup_inputs(seed: int = 0) -> dict:
    key = jax.random.key(seed)
    ks = jax.random.split(key, 13)
    f32 = jnp.float32

    def gain(k, n):
        return 1.0 + 0.05 * jax.random.normal(k, (DEPTH, n), f32)

    x = jax.random.normal(ks[0], (BATCH, SEQ, D_MODEL), f32)
    norm_pre = gain(ks[1], D_MODEL)
    norm_post = gain(ks[2], D_MODEL)
    w_in = jax.random.normal(ks[3], (DEPTH, D_MODEL, D_IN), f32) * D_MODEL ** -0.5
    a_q_norm = gain(ks[4], HEAD_DIM)
    a_k_norm = gain(ks[5], HEAD_DIM)
    b_q_norm = gain(ks[6], B_Q_LORA)
    b_kv_norm = gain(ks[7], B_KV_LORA)
    b_w_uq = jax.random.normal(ks[8], (DEPTH, B_Q_LORA, B_HEADS * (B_NOPE + B_ROPE)), f32) * B_Q_LORA ** -0.5
    b_w_ukv = jax.random.normal(ks[9], (DEPTH, B_KV_LORA, B_HEADS * (B_NOPE + B_V)), f32) * B_KV_LORA ** -0.5
    c_rpb = 0.1 * jax.random.normal(ks[10], (DEPTH, C_HEADS, 2 * WIN_ROWS - 1, 2 * WIN_COLS - 1), f32)
    w_out = jax.random.normal(ks[11], (DEPTH, D_MIX, D_MODEL), f32) * D_MIX ** -0.5
    return {"x": x, "norm_pre": norm_pre, "norm_post": norm_post, "w_in": w_in,
            "a_q_norm": a_q_norm, "a_k_norm": a_k_norm, "b_q_norm": b_q_norm,
            "b_kv_norm": b_kv_norm, "b_w_uq": b_w_uq, "b_w_ukv": b_w_ukv,
            "c_rpb": c_rpb, "w_out": w_out}


def reference(x, norm_pre, norm_post, w_in, a_q_norm, a_k_norm, b_q_norm, b_kv_norm,
              b_w_uq, b_w_ukv, c_rpb, w_out):
    S = x.shape[1]
    t = jnp.arange(S)
    row = t // GRID_W
    col = t % GRID_W
    h = x
    for l in range(DEPTH):
        h = _layer(h, norm_pre[l], norm_post[l], w_in[l], a_q_norm[l], a_k_norm[l],
                   b_q_norm[l], b_kv_norm[l], b_w_uq[l], b_w_ukv[l], c_rpb[l], w_out[l],
                   row, col)
    return h
```

```python
import functools

import jax
import jax.numpy as jnp
import numpy as np
from jax import lax
from jax.experimental import pallas as pl
from jax.experimental.pallas import tpu as pltpu

F32 = jnp.float32
BF16 = jnp.bfloat16

GRID_W = 64
HEAD_DIM = 128
A_HEADS = 8
A_KV_HEADS = 2
A_GROUP = A_HEADS // A_KV_HEADS
B_HEADS = 4
B_Q_LORA = 512
B_KV_LORA = 256
B_NOPE = 128
B_ROPE = 64
B_V = 128
B_QK_PAD = 256
C_HEADS = 4
WIN_ROWS = 8
WIN_COLS = 16
ROPE_THETA = 10000.0
NORM_EPS = 1e-6
NEG_BIG = -1e30

A_Q = A_HEADS * HEAD_DIM
A_KV = A_KV_HEADS * HEAD_DIM
B_G = B_HEADS * B_V
C_W = C_HEADS * HEAD_DIM
D_MIX = A_Q + B_G + C_W

OFF_AQ = 0
OFF_AK = OFF_AQ + A_Q
OFF_AV = OFF_AK + A_KV
OFF_AG = OFF_AV + A_KV
OFF_BCQ = OFF_AG + A_Q
OFF_BCKV = OFF_BCQ + B_Q_LORA
OFF_BG = OFF_BCKV + B_KV_LORA
OFF_CQ = OFF_BG + B_G
OFF_CK = OFF_CQ + C_W
OFF_CV = OFF_CK + C_W
OFF_CG = OFF_CV + C_W
OFF_BKR = OFF_CG + C_W
W_IN_COLS = OFF_BKR + 128

C_QROWS = 4
C_WROWS = 12
C_TQ = C_QROWS * GRID_W
C_TK = C_WROWS * GRID_W

VMEM_LIMIT_BYTES = 56 * 1024 * 1024


def _rms(x, gain):
    return x * lax.rsqrt(jnp.mean(x * x, axis=-1, keepdims=True) + NORM_EPS) * gain


def _rope(x, cos, sin_lo, sin_hi, half):
    up = pltpu.roll(x, 128 - half, 1)
    dn = pltpu.roll(x, half, 1)
    return x * cos + up * sin_lo + dn * sin_hi


def _silu(g):
    return g * (1.0 / (1.0 + jnp.exp(-g)))


def _inproj_kernel(x_ref, gpre_ref, w_ref, wuq_ref, wuk_ref, wuv_ref,
                   aqn_ref, akn_ref, bqn_ref, bkvn_ref,
                   cos_a, slo_a, shi_a, cos_b, slo_b, shi_b,
                   qa_ref, ka_ref, va_ref, qb_ref, kb_ref, vb_ref,
                   qc_ref, kc_ref, vc_ref, gate_ref, h_scr):
    x = x_ref[...]
    h_scr[...] = _rms(x, gpre_ref[...]).astype(BF16)

    def proj(off, width):
        return jnp.dot(h_scr[...], w_ref[:, off:off + width], preferred_element_type=F32)

    ca, sla, sha = cos_a[...], slo_a[...], shi_a[...]
    cb, slb, shb = cos_b[...], slo_b[...], shi_b[...]
    scale_a = HEAD_DIM ** -0.5
    scale_b = (B_NOPE + B_ROPE) ** -0.5

    pq = proj(OFF_AQ, A_Q)
    for hd in range(A_HEADS):
        sl = slice(hd * HEAD_DIM, (hd + 1) * HEAD_DIM)
        q = _rope(_rms(pq[:, sl], aqn_ref[...]), ca, sla, sha, 32)
        qa_ref[:, sl] = (q * scale_a).astype(BF16)
    pk = proj(OFF_AK, A_KV)
    for hd in range(A_KV_HEADS):
        sl = slice(hd * HEAD_DIM, (hd + 1) * HEAD_DIM)
        ka_ref[:, sl] = _rope(_rms(pk[:, sl], akn_ref[...]), ca, sla, sha, 32).astype(BF16)
    va_ref[...] = proj(OFF_AV, A_KV).astype(BF16)
    gate_ref[:, 0:A_Q] = _silu(proj(OFF_AG, A_Q)).astype(BF16)

    cq = _rms(proj(OFF_BCQ, B_Q_LORA), bqn_ref[...]).astype(BF16)
    qb = jnp.dot(cq, wuq_ref[...], preferred_element_type=F32)
    ckv = _rms(proj(OFF_BCKV, B_KV_LORA), bkvn_ref[...]).astype(BF16)
    kn = jnp.dot(ckv, wuk_ref[...], preferred_element_type=F32)
    vb_ref[...] = jnp.dot(ckv, wuv_ref[...], preferred_element_type=F32).astype(BF16)
    kpe = _rope(proj(OFF_BKR, 128), cb, slb, shb, 16).astype(BF16)
    for hd in range(B_HEADS):
        lo = hd * B_QK_PAD
        qb_ref[:, lo:lo + B_NOPE] = (qb[:, lo:lo + B_NOPE] * scale_b).astype(BF16)
        qpe = _rope(qb[:, lo + B_NOPE:lo + B_QK_PAD], cb, slb, shb, 16)
        qb_ref[:, lo + B_NOPE:lo + B_QK_PAD] = (qpe * scale_b).astype(BF16)
        kb_ref[:, lo:lo + B_NOPE] = kn[:, hd * B_NOPE:(hd + 1) * B_NOPE].astype(BF16)
        kb_ref[:, lo + B_NOPE:lo + B_QK_PAD] = kpe
    gate_ref[:, A_Q:A_Q + B_G] = _silu(proj(OFF_BG, B_G)).astype(BF16)

    qc_ref[...] = (proj(OFF_CQ, C_W) * scale_a).astype(BF16)
    kc_ref[...] = proj(OFF_CK, C_W).astype(BF16)
    vc_ref[...] = proj(OFF_CV, C_W).astype(BF16)
    gate_ref[:, A_Q + B_G:D_MIX] = _silu(proj(OFF_CG, C_W)).astype(BF16)


def _inproj(x2d, seq, gpre, w, wuq, wuk, wuv, aqn, akn, bqn, bkvn, tabs, tm):
    t, d = x2d.shape
    n_s = seq // tm
    row = lambda i: (i, 0)
    fix = lambda i: (0, 0)
    tab = lambda i: (i % n_s, 0)

    def resident(a):
        return pl.BlockSpec(a.shape, fix, pipeline_mode=pl.Buffered(1))

    out_widths = (A_Q, A_KV, A_KV, B_HEADS * B_QK_PAD, B_HEADS * B_QK_PAD, B_G, C_W, C_W, C_W, D_MIX)
    return pl.pallas_call(
        _inproj_kernel,
        grid=(t // tm,),
        in_specs=[pl.BlockSpec((tm, d), row), resident(gpre), resident(w), resident(wuq),
                  resident(wuk), resident(wuv), resident(aqn), resident(akn), resident(bqn),
                  resident(bkvn)] + [pl.BlockSpec((tm, 128), tab)] * 6,
        out_specs=[pl.BlockSpec((tm, n), row) for n in out_widths],
        out_shape=[jax.ShapeDtypeStruct((t, n), BF16) for n in out_widths],
        scratch_shapes=[pltpu.VMEM((tm, d), BF16)],
        compiler_params=pltpu.CompilerParams(dimension_semantics=("parallel",),
                                             vmem_limit_bytes=VMEM_LIMIT_BYTES),
        name="inproj",
    )(x2d, gpre, w, wuq, wuk, wuv, aqn, akn, bqn, bkvn, *tabs)


def _softmax_pv(s, v):
    m = jnp.max(s, axis=-1, keepdims=True)
    p = jnp.exp(s - m)
    l = jnp.sum(p, axis=-1, keepdims=True)
    o = jnp.dot(p.astype(BF16), v, preferred_element_type=F32)
    return o * (1.0 / l)


_NT = (((1,), (1,)), ((), ()))


def _dense_attn_kernel(q_ref, k_ref, v_ref, g_ref, o_ref, *, heads, dq, dv, tq):
    seq = q_ref.shape[0]
    k = k_ref[...]
    v = v_ref[...]
    for hd in range(heads):
        qs = slice(hd * dq, (hd + 1) * dq)
        os_ = slice(hd * dv, (hd + 1) * dv)

        def body(i, carry):
            r = pl.multiple_of(i * tq, tq)
            s = lax.dot_general(q_ref[pl.ds(r, tq), qs], k, _NT, preferred_element_type=F32)
            o = _softmax_pv(s, v) * g_ref[pl.ds(r, tq), os_].astype(F32)
            o_ref[pl.ds(r, tq), os_] = o.astype(BF16)
            return carry

        lax.fori_loop(0, seq // tq, body, 0)


def _dense_attn(q, k, v, gate, seq, *, heads, kv_heads, dq, dv, gate_off, tq, name):
    t = q.shape[0]
    group = heads // kv_heads
    g_blk = gate_off // (group * dv)
    kern = functools.partial(_dense_attn_kernel, heads=group, dq=dq, dv=dv, tq=tq)
    return pl.pallas_call(
        kern,
        grid=(t // seq, kv_heads),
        in_specs=[pl.BlockSpec((seq, group * dq), lambda b, h: (b, h)),
                  pl.BlockSpec((seq, dq), lambda b, h: (b, h)),
                  pl.BlockSpec((seq, dv), lambda b, h: (b, h)),
                  pl.BlockSpec((seq, group * dv), lambda b, h: (b, g_blk + h))],
        out_specs=pl.BlockSpec((seq, group * dv), lambda b, h: (b, h)),
        out_shape=jax.ShapeDtypeStruct((t, heads * dv), BF16),
        compiler_params=pltpu.CompilerParams(dimension_semantics=("parallel", "parallel"),
                                             vmem_limit_bytes=VMEM_LIMIT_BYTES),
        name=name,
    )(q, k, v, gate)


def _nbr_attn_kernel(q_ref, k_ref, v_ref, bias_ref, g_ref, o_ref, *, max_ws):
    i = pl.program_id(1)
    ws = jnp.clip(C_QROWS * i - WIN_ROWS // 2, 0, max_ws)
    start = pl.multiple_of(ws * GRID_W, GRID_W)
    for hd in range(C_HEADS):
        sl = slice(hd * HEAD_DIM, (hd + 1) * HEAD_DIM)
        kw = k_ref[pl.ds(start, C_TK), sl]
        vw = v_ref[pl.ds(start, C_TK), sl]
        s = lax.dot_general(q_ref[:, sl], kw, _NT, preferred_element_type=F32) + bias_ref[hd]
        o = _softmax_pv(s, vw) * g_ref[:, sl].astype(F32)
        o_ref[:, sl] = o.astype(BF16)


def _nbr_attn(q, k, v, bias, gate, seq, gate_off):
    t = q.shape[0]
    nb = seq // C_TQ
    g_blk = gate_off // C_W
    kern = functools.partial(_nbr_attn_kernel, max_ws=seq // GRID_W - C_WROWS)
    return pl.pallas_call(
        kern,
        grid=(t // seq, nb),
        in_specs=[pl.BlockSpec((C_TQ, C_W), lambda b, i: (b * nb + i, 0)),
                  pl.BlockSpec((seq, C_W), lambda b, i: (b, 0)),
                  pl.BlockSpec((seq, C_W), lambda b, i: (b, 0)),
                  pl.BlockSpec((None, C_HEADS, C_TQ, C_TK), lambda b, i: (i, 0, 0, 0)),
                  pl.BlockSpec((C_TQ, C_W), lambda b, i: (b * nb + i, g_blk))],
        out_specs=pl.BlockSpec((C_TQ, C_W), lambda b, i: (b * nb + i, 0)),
        out_shape=jax.ShapeDtypeStruct((t, C_W), BF16),
        compiler_params=pltpu.CompilerParams(dimension_semantics=("parallel", "arbitrary"),
                                             vmem_limit_bytes=VMEM_LIMIT_BYTES),
        name="attn_c",
    )(q, k, v, bias, gate)


def _nbr_bias(rpb, seq):
    rows = seq // GRID_W
    nb = seq // C_TQ
    blk = np.arange(nb)[:, None, None]
    qi = np.arange(C_TQ)[None, :, None]
    ki = np.arange(C_TK)[None, None, :]
    r = C_QROWS * blk + qi // GRID_W
    c = qi % GRID_W
    ws = np.clip(C_QROWS * blk - WIN_ROWS // 2, 0, rows - C_WROWS)
    kr = ws + ki // GRID_W
    kc = ki % GRID_W
    r0 = np.clip(r - WIN_ROWS // 2, 0, rows - WIN_ROWS)
    c0 = np.clip(c - WIN_COLS // 2, 0, GRID_W - WIN_COLS)
    valid = (kr >= r0) & (kr < r0 + WIN_ROWS) & (kc >= c0) & (kc < c0 + WIN_COLS)
    dr = np.clip(kr - r + (WIN_ROWS - 1), 0, 2 * WIN_ROWS - 2)
    dc = np.clip(kc - c + (WIN_COLS - 1), 0, 2 * WIN_COLS - 2)
    dr = np.broadcast_to(dr, valid.shape)
    dc = np.broadcast_to(dc, valid.shape)
    vals = rpb[:, dr, dc]
    bias = jnp.where(valid[None], vals, NEG_BIG)
    return jnp.transpose(bias, (1, 0, 2, 3))


def _outproj_kernel(ma_ref, mb_ref, mc_ref, x_ref, w_ref, gpost_ref, o_ref):
    y = jnp.dot(ma_ref[...], w_ref[0:A_Q, :], preferred_element_type=F32)
    y += jnp.dot(mb_ref[...], w_ref[A_Q:A_Q + B_G, :], preferred_element_type=F32)
    y += jnp.dot(mc_ref[...], w_ref[A_Q + B_G:D_MIX, :], preferred_element_type=F32)
    o_ref[...] = x_ref[...] + _rms(y, gpost_ref[...])


def _outproj(ma, mb, mc, x2d, w, gpost, tm):
    t, d = x2d.shape
    row = lambda i: (i, 0)
    fix = lambda i: (0, 0)
    return pl.pallas_call(
        _outproj_kernel,
        grid=(t // tm,),
        in_specs=[pl.BlockSpec((tm, A_Q), row), pl.BlockSpec((tm, B_G), row),
                  pl.BlockSpec((tm, C_W), row), pl.BlockSpec((tm, d), row),
                  pl.BlockSpec(w.shape, fix, pipeline_mode=pl.Buffered(1)),
                  pl.BlockSpec(gpost.shape, fix, pipeline_mode=pl.Buffered(1))],
        out_specs=pl.BlockSpec((tm, d), row),
        out_shape=jax.ShapeDtypeStruct((t, d), F32),
        compiler_params=pltpu.CompilerParams(dimension_semantics=("parallel",),
                                             vmem_limit_bytes=VMEM_LIMIT_BYTES),
        name="outproj",
    )(ma, mb, mc, x2d, w, gpost)


def _rope_tables(seq):
    t = np.arange(seq)
    pos = np.stack([t // GRID_W, t % GRID_W], axis=0).astype(np.float32)

    def tables(n, lanes_used):
        j = np.arange(128)
        axis = np.minimum(j // n, 1)
        i = j % (n // 2)
        inv_freq = 1.0 / (jnp.asarray(ROPE_THETA, F32) ** (jnp.arange(0, n, 2, dtype=F32) / n))
        ang = jnp.asarray(pos)[axis, :].T * inv_freq[i][None, :]
        used = jnp.asarray(j < lanes_used)[None, :]
        low = jnp.asarray((j % n) < n // 2)[None, :]
        cos = jnp.where(used, jnp.cos(ang), 0.0)
        sin = jnp.where(used, jnp.sin(ang), 0.0)
        return cos, jnp.where(low, -sin, 0.0), jnp.where(low, 0.0, sin)

    return tables(HEAD_DIM // 2, 128) + tables(B_ROPE // 2, B_ROPE)


def _prep_w_in(w):
    a = A_Q + 2 * A_KV + A_Q
    b_cq = a + B_Q_LORA
    b_ckv = b_cq + B_KV_LORA
    b_kr = b_ckv + B_ROPE
    b_g = b_kr + B_G
    d = w.shape[0]
    return jnp.concatenate(
        [w[:, :a], w[:, a:b_ckv], w[:, b_kr:b_g], w[:, b_g:], w[:, b_ckv:b_kr],
         jnp.zeros((d, 128 - B_ROPE), w.dtype)], axis=1).astype(BF16)


def _prep_w_uq(w):
    r = w.shape[0]
    w = w.reshape(r, B_HEADS, B_NOPE + B_ROPE)
    w = jnp.pad(w, ((0, 0), (0, 0), (0, B_QK_PAD - B_NOPE - B_ROPE)))
    return w.reshape(r, B_HEADS * B_QK_PAD).astype(BF16)


def _prep_w_ukv(w):
    r = w.shape[0]
    w = w.reshape(r, B_HEADS, B_NOPE + B_V)
    wk = w[:, :, :B_NOPE].reshape(r, B_HEADS * B_NOPE)
    wv = w[:, :, B_NOPE:].reshape(r, B_HEADS * B_V)
    return wk.astype(BF16), wv.astype(BF16)


def kernel(x, norm_pre, norm_post, w_in, a_q_norm, a_k_norm, b_q_norm, b_kv_norm, b_w_uq, b_w_ukv, c_rpb, w_out):
    batch, seq, d = x.shape
    depth = w_in.shape[0]
    assert seq % C_TQ == 0 and seq // GRID_W >= C_WROWS and w_in.shape[2] == W_IN_COLS - 64
    tabs = _rope_tables(seq)
    h = x.reshape(batch * seq, d)
    for l in range(depth):
        wuk, wuv = _prep_w_ukv(b_w_ukv[l])
        qa, ka, va, qb, kb, vb, qc, kc, vc, gate = _inproj(
            h, seq, norm_pre[l][None], _prep_w_in(w_in[l]), _prep_w_uq(b_w_uq[l]), wuk, wuv,
            a_q_norm[l][None], a_k_norm[l][None], b_q_norm[l][None], b_kv_norm[l][None],
            tabs, tm=256)
        mix_a = _dense_attn(qa, ka, va, gate, seq, heads=A_HEADS, kv_heads=A_KV_HEADS,
                            dq=HEAD_DIM, dv=HEAD_DIM, gate_off=0, tq=256, name="attn_a")
        mix_b = _dense_attn(qb, kb, vb, gate, seq, heads=B_HEADS, kv_heads=B_HEADS,
                            dq=B_QK_PAD, dv=B_V, gate_off=A_Q, tq=256, name="attn_b")
        mix_c = _nbr_attn(qc, kc, vc, _nbr_bias(c_rpb[l], seq), gate, seq, A_Q + B_G)
        h = _outproj(mix_a, mix_b, mix_c, h, w_out[l].astype(BF16), norm_post[l][None], tm=512)
    return h.reshape(batch, seq, d)
```

```python
import functools

import jax
import jax.numpy as jnp
import numpy as np
from jax import lax
from jax.experimental import pallas as pl
from jax.experimental.pallas import tpu as pltpu

F32 = jnp.float32
BF16 = jnp.bfloat16

GRID_W = 64
HEAD_DIM = 128
A_HEADS = 8
A_KV_HEADS = 2
A_GROUP = A_HEADS // A_KV_HEADS
B_HEADS = 4
B_Q_LORA = 512
B_KV_LORA = 256
B_NOPE = 128
B_ROPE = 64
B_V = 128
B_QK_PAD = 256
C_HEADS = 4
WIN_ROWS = 8
WIN_COLS = 16
ROPE_THETA = 10000.0
NORM_EPS = 1e-6
NEG_BIG = -1e30

A_Q = A_HEADS * HEAD_DIM
A_KV = A_KV_HEADS * HEAD_DIM
B_G = B_HEADS * B_V
C_W = C_HEADS * HEAD_DIM
D_MIX = A_Q + B_G + C_W

OFF_AQ = 0
OFF_AK = OFF_AQ + A_Q
OFF_AV = OFF_AK + A_KV
OFF_AG = OFF_AV + A_KV
OFF_BCQ = OFF_AG + A_Q
OFF_BCKV = OFF_BCQ + B_Q_LORA
OFF_BG = OFF_BCKV + B_KV_LORA
OFF_CQ = OFF_BG + B_G
OFF_CK = OFF_CQ + C_W
OFF_CV = OFF_CK + C_W
OFF_CG = OFF_CV + C_W
OFF_BKR = OFF_CG + C_W
W_IN_COLS = OFF_BKR + 128

C_QROWS = 4
C_WROWS = 12
C_TQ = C_QROWS * GRID_W
C_TK = C_WROWS * GRID_W

VMEM_LIMIT_BYTES = 56 * 1024 * 1024


def _rms(x, gain):
    return x * lax.rsqrt(jnp.mean(x * x, axis=-1, keepdims=True) + NORM_EPS) * gain


def _rope(x, cos, sin_lo, sin_hi, half):
    up = pltpu.roll(x, 128 - half, 1)
    dn = pltpu.roll(x, half, 1)
    return x * cos + up * sin_lo + dn * sin_hi


def _silu(g):
    return g * (1.0 / (1.0 + jnp.exp(-g)))


def _inproj_kernel(x_ref, gpre_ref, w_ref, wuq_ref, wuk_ref, wuv_ref,
                   aqn_ref, akn_ref, bqn_ref, bkvn_ref,
                   cos_a, slo_a, shi_a, cos_b, slo_b, shi_b,
                   qa_ref, ka_ref, va_ref, qb_ref, kb_ref, vb_ref,
                   qc_ref, kc_ref, vc_ref, gate_ref, h_scr):
    x = x_ref[...]
    h_scr[...] = _rms(x, gpre_ref[...]).astype(BF16)

    def proj(off, width):
        return jnp.dot(h_scr[...], w_ref[:, off:off + width], preferred_element_type=F32)

    ca, sla, sha = cos_a[...], slo_a[...], shi_a[...]
    cb, slb, shb = cos_b[...], slo_b[...], shi_b[...]
    scale_a = HEAD_DIM ** -0.5
    scale_b = (B_NOPE + B_ROPE) ** -0.5

    pq = proj(OFF_AQ, A_Q)
    for hd in range(A_HEADS):
        sl = slice(hd * HEAD_DIM, (hd + 1) * HEAD_DIM)
        q = _rope(_rms(pq[:, sl], aqn_ref[...]), ca, sla, sha, 32)
        qa_ref[:, sl] = (q * scale_a).astype(BF16)
    pk = proj(OFF_AK, A_KV)
    for hd in range(A_KV_HEADS):
        sl = slice(hd * HEAD_DIM, (hd + 1) * HEAD_DIM)
        ka_ref[:, sl] = _rope(_rms(pk[:, sl], akn_ref[...]), ca, sla, sha, 32).astype(BF16)
    va_ref[...] = proj(OFF_AV, A_KV).astype(BF16)
    gate_ref[:, 0:A_Q] = _silu(proj(OFF_AG, A_Q)).astype(BF16)

    cq = _rms(proj(OFF_BCQ, B_Q_LORA), bqn_ref[...]).astype(BF16)
    qb = jnp.dot(cq, wuq_ref[...], preferred_element_type=F32)
    ckv = _rms(proj(OFF_BCKV, B_KV_LORA), bkvn_ref[...]).astype(BF16)
    kn = jnp.dot(ckv, wuk_ref[...], preferred_element_type=F32)
    vb_ref[...] = jnp.dot(ckv, wuv_ref[...], preferred_element_type=F32).astype(BF16)
    kpe = _rope(proj(OFF_BKR, 128), cb, slb, shb, 16).astype(BF16)
    for hd in range(B_HEADS):
        lo = hd * B_QK_PAD
        qb_ref[:, lo:lo + B_NOPE] = (qb[:, lo:lo + B_NOPE] * scale_b).astype(BF16)
        qpe = _rope(qb[:, lo + B_NOPE:lo + B_QK_PAD], cb, slb, shb, 16)
        qb_ref[:, lo + B_NOPE:lo + B_QK_PAD] = (qpe * scale_b).astype(BF16)
        kb_ref[:, lo:lo + B_NOPE] = kn[:, hd * B_NOPE:(hd + 1) * B_NOPE].astype(BF16)
        kb_ref[:, lo + B_NOPE:lo + B_QK_PAD] = kpe
    gate_ref[:, A_Q:A_Q + B_G] = _silu(proj(OFF_BG, B_G)).astype(BF16)

    qc_ref[...] = (proj(OFF_CQ, C_W) * scale_a).astype(BF16)
    kc_ref[...] = proj(OFF_CK, C_W).astype(BF16)
    vc_ref[...] = proj(OFF_CV, C_W).astype(BF16)
    gate_ref[:, A_Q + B_G:D_MIX] = _silu(proj(OFF_CG, C_W)).astype(BF16)


def _inproj(x2d, seq, gpre, w, wuq, wuk, wuv, aqn, akn, bqn, bkvn, tabs, tm):
    t, d = x2d.shape
    n_s = seq // tm
    row = lambda i: (i, 0)
    fix = lambda i: (0, 0)
    tab = lambda i: (i % n_s, 0)

    def resident(a):
        return pl.BlockSpec(a.shape, fix, pipeline_mode=pl.Buffered(1))

    out_widths = (A_Q, A_KV, A_KV, B_HEADS * B_QK_PAD, B_HEADS * B_QK_PAD, B_G, C_W, C_W, C_W, D_MIX)
    return pl.pallas_call(
        _inproj_kernel,
        grid=(t // tm,),
        in_specs=[pl.BlockSpec((tm, d), row), resident(gpre), resident(w), resident(wuq),
                  resident(wuk), resident(wuv), resident(aqn), resident(akn), resident(bqn),
                  resident(bkvn)] + [pl.BlockSpec((tm, 128), tab)] * 6,
        out_specs=[pl.BlockSpec((tm, n), row) for n in out_widths],
        out_shape=[jax.ShapeDtypeStruct((t, n), BF16) for n in out_widths],
        scratch_shapes=[pltpu.VMEM((tm, d), BF16)],
        compiler_params=pltpu.CompilerParams(dimension_semantics=("parallel",),
                                             vmem_limit_bytes=VMEM_LIMIT_BYTES),
        name="inproj",
    )(x2d, gpre, w, wuq, wuk, wuv, aqn, akn, bqn, bkvn, *tabs)


def _softmax_pv(s, v):
    m = jnp.max(s, axis=-1, keepdims=True)
    p = jnp.exp(s - m)
    l = jnp.sum(p, axis=-1, keepdims=True)
    o = jnp.dot(p.astype(BF16), v, preferred_element_type=F32)
    return o * (1.0 / l)


_NT = (((1,), (1,)), ((), ()))


def _dense_attn_kernel(q_ref, k_ref, v_ref, g_ref, o_ref, *, heads, dq, dv, tq):
    seq = q_ref.shape[0]
    k = k_ref[...]
    v = v_ref[...]
    for hd in range(heads):
        qs = slice(hd * dq, (hd + 1) * dq)
        os_ = slice(hd * dv, (hd + 1) * dv)

        def body(i, carry):
            r = pl.multiple_of(i * tq, tq)
            s = lax.dot_general(q_ref[pl.ds(r, tq), qs], k, _NT, preferred_element_type=F32)
            o = _softmax_pv(s, v) * g_ref[pl.ds(r, tq), os_].astype(F32)
            o_ref[pl.ds(r, tq), os_] = o.astype(BF16)
            return carry

        lax.fori_loop(0, seq // tq, body, 0)


def _dense_attn(q, k, v, gate, seq, *, heads, kv_heads, dq, dv, gate_off, tq, name):
    t = q.shape[0]
    group = heads // kv_heads
    g_blk = gate_off // (group * dv)
    kern = functools.partial(_dense_attn_kernel, heads=group, dq=dq, dv=dv, tq=tq)
    return pl.pallas_call(
        kern,
        grid=(t // seq, kv_heads),
        in_specs=[pl.BlockSpec((seq, group * dq), lambda b, h: (b, h)),
                  pl.BlockSpec((seq, dq), lambda b, h: (b, h)),
                  pl.BlockSpec((seq, dv), lambda b, h: (b, h)),
                  pl.BlockSpec((seq, group * dv), lambda b, h: (b, g_blk + h))],
        out_specs=pl.BlockSpec((seq, group * dv), lambda b, h: (b, h)),
        out_shape=jax.ShapeDtypeStruct((t, heads * dv), BF16),
        compiler_params=pltpu.CompilerParams(dimension_semantics=("parallel", "parallel"),
                                             vmem_limit_bytes=VMEM_LIMIT_BYTES),
        name=name,
    )(q, k, v, gate)


def _nbr_attn_kernel(pid_ref, q_ref, k_ref, v_ref, bias_ref, g_ref, o_ref, *, max_ws):
    del pid_ref
    i = pl.program_id(1)
    ws = jnp.clip(C_QROWS * i - WIN_ROWS // 2, 0, max_ws)
    start = pl.multiple_of(ws * GRID_W, GRID_W)
    for hd in range(C_HEADS):
        sl = slice(hd * HEAD_DIM, (hd + 1) * HEAD_DIM)
        kw = k_ref[pl.ds(start, C_TK), sl]
        vw = v_ref[pl.ds(start, C_TK), sl]
        s = lax.dot_general(q_ref[:, sl], kw, _NT, preferred_element_type=F32) + bias_ref[hd]
        o = _softmax_pv(s, vw) * g_ref[:, sl].astype(F32)
        o_ref[:, sl] = o.astype(BF16)


def _nbr_attn(q, k, v, bias, pat_ids, gate, seq, gate_off):
    t = q.shape[0]
    nb = seq // C_TQ
    g_blk = gate_off // C_W
    kern = functools.partial(_nbr_attn_kernel, max_ws=seq // GRID_W - C_WROWS)
    grid_spec = pltpu.PrefetchScalarGridSpec(
        num_scalar_prefetch=1,
        grid=(t // seq, nb),
        in_specs=[pl.BlockSpec((C_TQ, C_W), lambda b, i, pid: (b * nb + i, 0)),
                  pl.BlockSpec((seq, C_W), lambda b, i, pid: (b, 0)),
                  pl.BlockSpec((seq, C_W), lambda b, i, pid: (b, 0)),
                  pl.BlockSpec((None, C_HEADS, C_TQ, C_TK), lambda b, i, pid: (pid[i], 0, 0, 0)),
                  pl.BlockSpec((C_TQ, C_W), lambda b, i, pid: (b * nb + i, g_blk))],
        out_specs=pl.BlockSpec((C_TQ, C_W), lambda b, i, pid: (b * nb + i, 0)))
    return pl.pallas_call(
        kern,
        grid_spec=grid_spec,
        out_shape=jax.ShapeDtypeStruct((t, C_W), BF16),
        compiler_params=pltpu.CompilerParams(dimension_semantics=("parallel", "arbitrary"),
                                             vmem_limit_bytes=VMEM_LIMIT_BYTES),
        name="attn_c",
    )(pat_ids, q, k, v, bias, gate)


def _nbr_bias(rpb, seq):
    heads, n_dr, n_dc = rpb.shape
    rows = seq // GRID_W
    nb = seq // C_TQ
    lead = GRID_W - WIN_COLS
    u = jnp.pad(rpb, ((0, 0), (0, 0), (lead, 2 * GRID_W - n_dc - lead)))
    skew = jnp.tile(u, (1, 1, GRID_W))[:, :, GRID_W - 1:GRID_W - 1 + GRID_W * (2 * GRID_W - 1)]
    vals = skew.reshape(heads, n_dr, GRID_W, 2 * GRID_W - 1)[..., :GRID_W]
    qc = np.arange(GRID_W)[:, None]
    kc = np.arange(GRID_W)[None, :]
    c0 = np.clip(qc - WIN_COLS // 2, 0, GRID_W - WIN_COLS)
    valid_c = (kc >= c0) & (kc < c0 + WIN_COLS)
    slabs = jnp.where(valid_c[None, None], vals, NEG_BIG)
    slabs = jnp.concatenate([slabs, jnp.full((heads, 1, GRID_W, GRID_W), NEG_BIG, rpb.dtype)], axis=1)
    blk = np.arange(nb)[:, None, None]
    r = C_QROWS * blk + np.arange(C_QROWS)[None, :, None]
    ws = np.clip(C_QROWS * blk - WIN_ROWS // 2, 0, rows - C_WROWS)
    kr = ws + np.arange(C_WROWS)[None, None, :]
    r0 = np.clip(r - WIN_ROWS // 2, 0, rows - WIN_ROWS)
    valid_r = (kr >= r0) & (kr < r0 + WIN_ROWS)
    slab_id = np.where(valid_r, kr - r + (WIN_ROWS - 1), n_dr)
    pats, pat_ids = np.unique(slab_id.reshape(nb, -1), axis=0, return_inverse=True)
    picked = jnp.stack([slabs[:, int(s)] for s in pats.reshape(-1)], axis=0)
    picked = picked.reshape(pats.shape[0], C_QROWS, C_WROWS, heads, GRID_W, GRID_W)
    bias = jnp.transpose(picked, (0, 3, 1, 4, 2, 5)).reshape(pats.shape[0], heads, C_TQ, C_TK)
    return bias, jnp.asarray(pat_ids.reshape(nb), jnp.int32)


def _outproj_kernel(ma_ref, mb_ref, mc_ref, x_ref, w_ref, gpost_ref, o_ref):
    y = jnp.dot(ma_ref[...], w_ref[0:A_Q, :], preferred_element_type=F32)
    y += jnp.dot(mb_ref[...], w_ref[A_Q:A_Q + B_G, :], preferred_element_type=F32)
    y += jnp.dot(mc_ref[...], w_ref[A_Q + B_G:D_MIX, :], preferred_element_type=F32)
    o_ref[...] = x_ref[...] + _rms(y, gpost_ref[...])


def _outproj(ma, mb, mc, x2d, w, gpost, tm):
    t, d = x2d.shape
    row = lambda i: (i, 0)
    fix = lambda i: (0, 0)
    return pl.pallas_call(
        _outproj_kernel,
        grid=(t // tm,),
        in_specs=[pl.BlockSpec((tm, A_Q), row), pl.BlockSpec((tm, B_G), row),
                  pl.BlockSpec((tm, C_W), row), pl.BlockSpec((tm, d), row),
                  pl.BlockSpec(w.shape, fix, pipeline_mode=pl.Buffered(1)),
                  pl.BlockSpec(gpost.shape, fix, pipeline_mode=pl.Buffered(1))],
        out_specs=pl.BlockSpec((tm, d), row),
        out_shape=jax.ShapeDtypeStruct((t, d), F32),
        compiler_params=pltpu.CompilerParams(dimension_semantics=("parallel",),
                                             vmem_limit_bytes=VMEM_LIMIT_BYTES),
        name="outproj",
    )(ma, mb, mc, x2d, w, gpost)


def _rope_tables(seq):
    t = np.arange(seq)
    pos = np.stack([t // GRID_W, t % GRID_W], axis=0).astype(np.float32)

    def tables(n, lanes_used):
        j = np.arange(128)
        axis = np.minimum(j // n, 1)
        i = j % (n // 2)
        inv_freq = 1.0 / (jnp.asarray(ROPE_THETA, F32) ** (jnp.arange(0, n, 2, dtype=F32) / n))
        ang = jnp.asarray(pos)[axis, :].T * inv_freq[i][None, :]
        used = jnp.asarray(j < lanes_used)[None, :]
        low = jnp.asarray((j % n) < n // 2)[None, :]
        cos = jnp.where(used, jnp.cos(ang), 0.0)
        sin = jnp.where(used, jnp.sin(ang), 0.0)
        return cos, jnp.where(low, -sin, 0.0), jnp.where(low, 0.0, sin)

    return tables(HEAD_DIM // 2, 128) + tables(B_ROPE // 2, B_ROPE)


def _prep_w_in(w):
    a = A_Q + 2 * A_KV + A_Q
    b_cq = a + B_Q_LORA
    b_ckv = b_cq + B_KV_LORA
    b_kr = b_ckv + B_ROPE
    b_g = b_kr + B_G
    d = w.shape[0]
    return jnp.concatenate(
        [w[:, :a], w[:, a:b_ckv], w[:, b_kr:b_g], w[:, b_g:], w[:, b_ckv:b_kr],
         jnp.zeros((d, 128 - B_ROPE), w.dtype)], axis=1).astype(BF16)


def _prep_w_uq(w):
    r = w.shape[0]
    w = w.reshape(r, B_HEADS, B_NOPE + B_ROPE)
    w = jnp.pad(w, ((0, 0), (0, 0), (0, B_QK_PAD - B_NOPE - B_ROPE)))
    return w.reshape(r, B_HEADS * B_QK_PAD).astype(BF16)


def _prep_w_ukv(w):
    r = w.shape[0]
    w = w.reshape(r, B_HEADS, B_NOPE + B_V)
    wk = w[:, :, :B_NOPE].reshape(r, B_HEADS * B_NOPE)
    wv = w[:, :, B_NOPE:].reshape(r, B_HEADS * B_V)
    return wk.astype(BF16), wv.astype(BF16)


def kernel(x, norm_pre, norm_post, w_in, a_q_norm, a_k_norm, b_q_norm, b_kv_norm, b_w_uq, b_w_ukv, c_rpb, w_out):
    batch, seq, d = x.shape
    depth = w_in.shape[0]
    assert seq % C_TQ == 0 and seq // GRID_W >= C_WROWS and w_in.shape[2] == W_IN_COLS - 64
    tabs = _rope_tables(seq)
    h = x.reshape(batch * seq, d)
    for l in range(depth):
        wuk, wuv = _prep_w_ukv(b_w_ukv[l])
        qa, ka, va, qb, kb, vb, qc, kc, vc, gate = _inproj(
            h, seq, norm_pre[l][None], _prep_w_in(w_in[l]), _prep_w_uq(b_w_uq[l]), wuk, wuv,
            a_q_norm[l][None], a_k_norm[l][None], b_q_norm[l][None], b_kv_norm[l][None],
            tabs, tm=256)
        mix_a = _dense_attn(qa, ka, va, gate, seq, heads=A_HEADS, kv_heads=A_KV_HEADS,
                            dq=HEAD_DIM, dv=HEAD_DIM, gate_off=0, tq=256, name="attn_a")
        mix_b = _dense_attn(qb, kb, vb, gate, seq, heads=B_HEADS, kv_heads=B_HEADS,
                            dq=B_QK_PAD, dv=B_V, gate_off=A_Q, tq=256, name="attn_b")
        bias, pat_ids = _nbr_bias(c_rpb[l], seq)
        mix_c = _nbr_attn(qc, kc, vc, bias, pat_ids, gate, seq, A_Q + B_G)
        h = _outproj(mix_a, mix_b, mix_c, h, w_out[l].astype(BF16), norm_post[l][None], tm=512)
    return h.reshape(batch, seq, d)
```

```python
import functools

import jax
import jax.numpy as jnp
import numpy as np
from jax import lax
from jax.experimental import pallas as pl
from jax.experimental.pallas import tpu as pltpu

F32 = jnp.float32
BF16 = jnp.bfloat16

GRID_W = 64
HEAD_DIM = 128
A_HEADS = 8
A_KV_HEADS = 2
A_GROUP = A_HEADS // A_KV_HEADS
B_HEADS = 4
B_Q_LORA = 512
B_KV_LORA = 256
B_NOPE = 128
B_ROPE = 64
B_V = 128
B_QK_PAD = 256
C_HEADS = 4
WIN_ROWS = 8
WIN_COLS = 16
ROPE_THETA = 10000.0
NORM_EPS = 1e-6
NEG_BIG = -1e30
LOG2E = 1.4426950408889634

A_Q = A_HEADS * HEAD_DIM
A_KV = A_KV_HEADS * HEAD_DIM
B_G = B_HEADS * B_V
C_W = C_HEADS * HEAD_DIM
D_MIX = A_Q + B_G + C_W

OFF_AQ = 0
OFF_AK = OFF_AQ + A_Q
OFF_AV = OFF_AK + A_KV
OFF_AG = OFF_AV + A_KV
OFF_BCQ = OFF_AG + A_Q
OFF_BCKV = OFF_BCQ + B_Q_LORA
OFF_BG = OFF_BCKV + B_KV_LORA
OFF_CQ = OFF_BG + B_G
OFF_CK = OFF_CQ + C_W
OFF_CV = OFF_CK + C_W
OFF_CG = OFF_CV + C_W
OFF_BKR = OFF_CG + C_W
W_IN_COLS = OFF_BKR + 128

C_QROWS = 4
C_WROWS = 12
C_TQ = C_QROWS * GRID_W
C_TK = C_WROWS * GRID_W

VMEM_LIMIT_BYTES = 56 * 1024 * 1024


def _rms(x, gain):
    return x * lax.rsqrt(jnp.mean(x * x, axis=-1, keepdims=True) + NORM_EPS) * gain


def _rope(x, cos, sin_lo, sin_hi, half):
    up = pltpu.roll(x, 128 - half, 1)
    dn = pltpu.roll(x, half, 1)
    return x * cos + up * sin_lo + dn * sin_hi


def _silu(g):
    return g * (1.0 / (1.0 + jnp.exp(-g)))


def _inproj_kernel(x_ref, gpre_ref, w_ref, wuq_ref, wuk_ref, wuv_ref,
                   aqn_ref, akn_ref, bqn_ref, bkvn_ref,
                   cos_a, slo_a, shi_a, cos_b, slo_b, shi_b,
                   qa_ref, ka_ref, va_ref, qb_ref, kb_ref, vb_ref,
                   qc_ref, kc_ref, vc_ref, gate_ref, h_scr):
    x = x_ref[...]
    h_scr[...] = (x * gpre_ref[...]).astype(BF16)
    inv_rms = lax.rsqrt(jnp.mean(x * x, axis=-1, keepdims=True) + NORM_EPS)
    tm = x.shape[0]

    def proj(off, width):
        return jnp.dot(h_scr[...], w_ref[:, off:off + width], preferred_element_type=F32) * inv_rms

    def with_ones(ref, hd, v):
        ref[:, 2 * hd * HEAD_DIM:(2 * hd + 1) * HEAD_DIM] = v.astype(BF16)
        ref[:, (2 * hd + 1) * HEAD_DIM:(2 * hd + 2) * HEAD_DIM] = jnp.ones((tm, HEAD_DIM), BF16)

    ca, sla, sha = cos_a[...], slo_a[...], shi_a[...]
    cb, slb, shb = cos_b[...], slo_b[...], shi_b[...]
    scale_a = HEAD_DIM ** -0.5 * LOG2E
    scale_b = (B_NOPE + B_ROPE) ** -0.5 * LOG2E

    cq = _rms(proj(OFF_BCQ, B_Q_LORA), bqn_ref[...]).astype(BF16)
    ckv = _rms(proj(OFF_BCKV, B_KV_LORA), bkvn_ref[...]).astype(BF16)
    kpe = _rope(proj(OFF_BKR, 128), cb, slb, shb, 16).astype(BF16)

    pq = proj(OFF_AQ, A_Q)
    for hd in range(A_HEADS):
        sl = slice(hd * HEAD_DIM, (hd + 1) * HEAD_DIM)
        q = _rope(_rms(pq[:, sl], aqn_ref[...]), ca, sla, sha, 32)
        qa_ref[:, sl] = (q * scale_a).astype(BF16)
    pk = proj(OFF_AK, A_KV)
    pv = proj(OFF_AV, A_KV)
    for hd in range(A_KV_HEADS):
        sl = slice(hd * HEAD_DIM, (hd + 1) * HEAD_DIM)
        ka_ref[:, sl] = _rope(_rms(pk[:, sl], akn_ref[...]), ca, sla, sha, 32).astype(BF16)
        with_ones(va_ref, hd, pv[:, sl])
    gate_ref[:, 0:A_Q] = _silu(proj(OFF_AG, A_Q)).astype(BF16)

    qb = jnp.dot(cq, wuq_ref[...], preferred_element_type=F32)
    kn = jnp.dot(ckv, wuk_ref[...], preferred_element_type=F32)
    vb = jnp.dot(ckv, wuv_ref[...], preferred_element_type=F32)
    for hd in range(B_HEADS):
        lo = hd * B_QK_PAD
        qb_ref[:, lo:lo + B_NOPE] = (qb[:, lo:lo + B_NOPE] * scale_b).astype(BF16)
        qpe = _rope(qb[:, lo + B_NOPE:lo + B_QK_PAD], cb, slb, shb, 16)
        qb_ref[:, lo + B_NOPE:lo + B_QK_PAD] = (qpe * scale_b).astype(BF16)
        kb_ref[:, lo:lo + B_NOPE] = kn[:, hd * B_NOPE:(hd + 1) * B_NOPE].astype(BF16)
        kb_ref[:, lo + B_NOPE:lo + B_QK_PAD] = kpe
        with_ones(vb_ref, hd, vb[:, hd * B_V:(hd + 1) * B_V])
    gate_ref[:, A_Q:A_Q + B_G] = _silu(proj(OFF_BG, B_G)).astype(BF16)

    gate_ref[:, A_Q + B_G:D_MIX] = _silu(proj(OFF_CG, C_W)).astype(BF16)
    qc_ref[...] = (proj(OFF_CQ, C_W) * scale_a).astype(BF16)
    pvc = proj(OFF_CV, C_W)
    for hd in range(C_HEADS):
        with_ones(vc_ref, hd, pvc[:, hd * HEAD_DIM:(hd + 1) * HEAD_DIM])
    kc_ref[...] = proj(OFF_CK, C_W).astype(BF16)


def _inproj(x2d, seq, gpre, w, wuq, wuk, wuv, aqn, akn, bqn, bkvn, tabs, tm):
    t, d = x2d.shape
    n_s = seq // tm
    row = lambda i: (i, 0)
    fix = lambda i: (0, 0)
    tab = lambda i: (i % n_s, 0)

    def resident(a):
        return pl.BlockSpec(a.shape, fix, pipeline_mode=pl.Buffered(1))

    out_widths = (A_Q, A_KV, 2 * A_KV, B_HEADS * B_QK_PAD, B_HEADS * B_QK_PAD, 2 * B_G,
                  C_W, C_W, 2 * C_W, D_MIX)
    return pl.pallas_call(
        _inproj_kernel,
        grid=(t // tm,),
        in_specs=[pl.BlockSpec((tm, d), row), resident(gpre), resident(w), resident(wuq),
                  resident(wuk), resident(wuv), resident(aqn), resident(akn), resident(bqn),
                  resident(bkvn)] + [pl.BlockSpec((tm, 128), tab)] * 6,
        out_specs=[pl.BlockSpec((tm, n), row) for n in out_widths],
        out_shape=[jax.ShapeDtypeStruct((t, n), BF16) for n in out_widths],
        scratch_shapes=[pltpu.VMEM((tm, d), BF16)],
        compiler_params=pltpu.CompilerParams(dimension_semantics=("parallel",),
                                             vmem_limit_bytes=VMEM_LIMIT_BYTES),
        name="inproj",
    )(x2d, gpre, w, wuq, wuk, wuv, aqn, akn, bqn, bkvn, *tabs)


def _softmax_pv(s, v_ext, dv):
    m = jnp.max(s, axis=-1, keepdims=True)
    p = jnp.exp2(s - m).astype(BF16)
    oe = jnp.dot(p, v_ext, preferred_element_type=F32)
    return oe[:, :dv] * (1.0 / oe[:, dv:])


_NT = (((1,), (1,)), ((), ()))


def _dense_attn_kernel(q_ref, k_ref, v_ref, g_ref, o_ref, s_a, s_b, *, heads, shared_kv, dq, dv, tq):
    seq = q_ref.shape[0]

    def qk(hd, r, s_ref):
        kh = 0 if shared_kv else hd
        s_ref[...] = lax.dot_general(q_ref[pl.ds(r, tq), hd * dq:(hd + 1) * dq],
                                     k_ref[:, kh * dq:(kh + 1) * dq], _NT,
                                     preferred_element_type=F32)

    def finish(hd, r, s_ref):
        kh = 0 if shared_kv else hd
        o = _softmax_pv(s_ref[...], v_ref[:, kh * 2 * dv:(kh + 1) * 2 * dv], dv)
        g = g_ref[pl.ds(r, tq), hd * dv:(hd + 1) * dv].astype(F32)
        o_ref[pl.ds(r, tq), hd * dv:(hd + 1) * dv] = (o * g).astype(BF16)

    qk(0, 0, s_a)
    for hd in range(heads):

        def body(j, carry):
            r0 = pl.multiple_of(j * 2 * tq, 2 * tq)
            r1 = pl.multiple_of(r0 + tq, tq)
            qk(hd, r1, s_b)
            finish(hd, r0, s_a)
            qk(hd, pl.multiple_of(r0 + 2 * tq, 2 * tq), s_a)
            finish(hd, r1, s_b)
            return carry

        lax.fori_loop(0, seq // (2 * tq) - 1, body, 0)
        r0 = seq - 2 * tq
        qk(hd, r0 + tq, s_b)
        finish(hd, r0, s_a)
        if hd + 1 < heads:
            qk(hd + 1, 0, s_a)
        finish(hd, r0 + tq, s_b)


def _dense_attn(q, k, v_ext, gate, seq, *, heads, kv_heads, dq, dv, gate_off, tq, name):
    t = q.shape[0]
    shared_kv = kv_heads < heads
    per_step = heads // kv_heads if shared_kv else heads
    n_steps = heads // per_step
    kv_per_step = 1 if shared_kv else per_step
    g_blk = gate_off // (per_step * dv)
    kern = functools.partial(_dense_attn_kernel, heads=per_step, shared_kv=shared_kv,
                             dq=dq, dv=dv, tq=tq)
    return pl.pallas_call(
        kern,
        grid=(t // seq, n_steps),
        in_specs=[pl.BlockSpec((seq, per_step * dq), lambda b, h: (b, h)),
                  pl.BlockSpec((seq, kv_per_step * dq), lambda b, h: (b, h)),
                  pl.BlockSpec((seq, kv_per_step * 2 * dv), lambda b, h: (b, h)),
                  pl.BlockSpec((seq, per_step * dv), lambda b, h: (b, g_blk + h))],
        out_specs=pl.BlockSpec((seq, per_step * dv), lambda b, h: (b, h)),
        out_shape=jax.ShapeDtypeStruct((t, heads * dv), BF16),
        scratch_shapes=[pltpu.VMEM((tq, seq), F32), pltpu.VMEM((tq, seq), F32)],
        compiler_params=pltpu.CompilerParams(dimension_semantics=("parallel", "parallel"),
                                             vmem_limit_bytes=VMEM_LIMIT_BYTES),
        name=name,
    )(q, k, v_ext, gate)


def _nbr_attn_kernel(pid_ref, q_ref, k_ref, v_ref, bias_ref, g_ref, o_ref, *, max_ws):
    del pid_ref
    i = pl.program_id(1)
    ws = jnp.clip(C_QROWS * i - WIN_ROWS // 2, 0, max_ws)
    start = pl.multiple_of(ws * GRID_W, GRID_W)
    for hd in range(C_HEADS):
        sl = slice(hd * HEAD_DIM, (hd + 1) * HEAD_DIM)
        kw = k_ref[pl.ds(start, C_TK), sl]
        vw = v_ref[pl.ds(start, C_TK), 2 * hd * HEAD_DIM:2 * (hd + 1) * HEAD_DIM]
        s = lax.dot_general(q_ref[:, sl], kw, _NT, preferred_element_type=F32) + bias_ref[hd]
        o = _softmax_pv(s, vw, HEAD_DIM) * g_ref[:, sl].astype(F32)
        o_ref[:, sl] = o.astype(BF16)


def _nbr_attn(q, k, v, bias, pat_ids, gate, seq, gate_off):
    t = q.shape[0]
    nb = seq // C_TQ
    g_blk = gate_off // C_W
    kern = functools.partial(_nbr_attn_kernel, max_ws=seq // GRID_W - C_WROWS)
    grid_spec = pltpu.PrefetchScalarGridSpec(
        num_scalar_prefetch=1,
        grid=(t // seq, nb),
        in_specs=[pl.BlockSpec((C_TQ, C_W), lambda b, i, pid: (b * nb + i, 0)),
                  pl.BlockSpec((seq, C_W), lambda b, i, pid: (b, 0)),
                  pl.BlockSpec((seq, 2 * C_W), lambda b, i, pid: (b, 0)),
                  pl.BlockSpec((None, C_HEADS, C_TQ, C_TK), lambda b, i, pid: (pid[i], 0, 0, 0)),
                  pl.BlockSpec((C_TQ, C_W), lambda b, i, pid: (b * nb + i, g_blk))],
        out_specs=pl.BlockSpec((C_TQ, C_W), lambda b, i, pid: (b * nb + i, 0)))
    return pl.pallas_call(
        kern,
        grid_spec=grid_spec,
        out_shape=jax.ShapeDtypeStruct((t, C_W), BF16),
        compiler_params=pltpu.CompilerParams(dimension_semantics=("parallel", "arbitrary"),
                                             vmem_limit_bytes=VMEM_LIMIT_BYTES),
        name="attn_c",
    )(pat_ids, q, k, v, bias, gate)


def _nbr_bias(rpb, seq):
    heads, n_dr, n_dc = rpb.shape
    rows = seq // GRID_W
    nb = seq // C_TQ
    lead = GRID_W - WIN_COLS
    u = jnp.pad(rpb, ((0, 0), (0, 0), (lead, 2 * GRID_W - n_dc - lead)))
    skew = jnp.tile(u, (1, 1, GRID_W))[:, :, GRID_W - 1:GRID_W - 1 + GRID_W * (2 * GRID_W - 1)]
    vals = skew.reshape(heads, n_dr, GRID_W, 2 * GRID_W - 1)[..., :GRID_W]
    qc = np.arange(GRID_W)[:, None]
    kc = np.arange(GRID_W)[None, :]
    c0 = np.clip(qc - WIN_COLS // 2, 0, GRID_W - WIN_COLS)
    valid_c = (kc >= c0) & (kc < c0 + WIN_COLS)
    slabs = jnp.where(valid_c[None, None], vals, NEG_BIG)
    slabs = jnp.concatenate([slabs, jnp.full((heads, 1, GRID_W, GRID_W), NEG_BIG, rpb.dtype)], axis=1)
    blk = np.arange(nb)[:, None, None]
    r = C_QROWS * blk + np.arange(C_QROWS)[None, :, None]
    ws = np.clip(C_QROWS * blk - WIN_ROWS // 2, 0, rows - C_WROWS)
    kr = ws + np.arange(C_WROWS)[None, None, :]
    r0 = np.clip(r - WIN_ROWS // 2, 0, rows - WIN_ROWS)
    valid_r = (kr >= r0) & (kr < r0 + WIN_ROWS)
    slab_id = np.where(valid_r, kr - r + (WIN_ROWS - 1), n_dr)
    pats, pat_ids = np.unique(slab_id.reshape(nb, -1), axis=0, return_inverse=True)
    picked = jnp.stack([slabs[:, int(s)] for s in pats.reshape(-1)], axis=0)
    picked = picked.reshape(pats.shape[0], C_QROWS, C_WROWS, heads, GRID_W, GRID_W)
    bias = jnp.transpose(picked, (0, 3, 1, 4, 2, 5)).reshape(pats.shape[0], heads, C_TQ, C_TK)
    return bias * LOG2E, jnp.asarray(pat_ids.reshape(nb), jnp.int32)


def _outproj_kernel(ma_ref, mb_ref, mc_ref, x_ref, w_ref, gpost_ref, o_ref):
    y = jnp.dot(ma_ref[...], w_ref[0:A_Q, :], preferred_element_type=F32)
    y += jnp.dot(mb_ref[...], w_ref[A_Q:A_Q + B_G, :], preferred_element_type=F32)
    y += jnp.dot(mc_ref[...], w_ref[A_Q + B_G:D_MIX, :], preferred_element_type=F32)
    o_ref[...] = x_ref[...] + _rms(y, gpost_ref[...])


def _outproj(ma, mb, mc, x2d, w, gpost, tm):
    t, d = x2d.shape
    row = lambda i: (i, 0)
    fix = lambda i: (0, 0)
    return pl.pallas_call(
        _outproj_kernel,
        grid=(t // tm,),
        in_specs=[pl.BlockSpec((tm, A_Q), row), pl.BlockSpec((tm, B_G), row),
                  pl.BlockSpec((tm, C_W), row), pl.BlockSpec((tm, d), row),
                  pl.BlockSpec(w.shape, fix, pipeline_mode=pl.Buffered(1)),
                  pl.BlockSpec(gpost.shape, fix, pipeline_mode=pl.Buffered(1))],
        out_specs=pl.BlockSpec((tm, d), row),
        out_shape=jax.ShapeDtypeStruct((t, d), F32),
        compiler_params=pltpu.CompilerParams(dimension_semantics=("parallel",),
                                             vmem_limit_bytes=VMEM_LIMIT_BYTES),
        name="outproj",
    )(ma, mb, mc, x2d, w, gpost)


def _rope_tables(seq):
    t = np.arange(seq)
    pos = np.stack([t // GRID_W, t % GRID_W], axis=0).astype(np.float32)

    def tables(n, lanes_used):
        j = np.arange(128)
        axis = np.minimum(j // n, 1)
        i = j % (n // 2)
        inv_freq = 1.0 / (jnp.asarray(ROPE_THETA, F32) ** (jnp.arange(0, n, 2, dtype=F32) / n))
        ang = jnp.asarray(pos)[axis, :].T * inv_freq[i][None, :]
        used = jnp.asarray(j < lanes_used)[None, :]
        low = jnp.asarray((j % n) < n // 2)[None, :]
        cos = jnp.where(used, jnp.cos(ang), 0.0)
        sin = jnp.where(used, jnp.sin(ang), 0.0)
        return cos, jnp.where(low, -sin, 0.0), jnp.where(low, 0.0, sin)

    return tables(HEAD_DIM // 2, 128) + tables(B_ROPE // 2, B_ROPE)


def _prep_w_in(w):
    a = A_Q + 2 * A_KV + A_Q
    b_cq = a + B_Q_LORA
    b_ckv = b_cq + B_KV_LORA
    b_kr = b_ckv + B_ROPE
    b_g = b_kr + B_G
    d = w.shape[0]
    return jnp.concatenate(
        [w[:, :a], w[:, a:b_ckv], w[:, b_kr:b_g], w[:, b_g:], w[:, b_ckv:b_kr],
         jnp.zeros((d, 128 - B_ROPE), w.dtype)], axis=1).astype(BF16)


def _prep_w_uq(w):
    r = w.shape[0]
    w = w.reshape(r, B_HEADS, B_NOPE + B_ROPE)
    w = jnp.pad(w, ((0, 0), (0, 0), (0, B_QK_PAD - B_NOPE - B_ROPE)))
    return w.reshape(r, B_HEADS * B_QK_PAD).astype(BF16)


def _prep_w_ukv(w):
    r = w.shape[0]
    w = w.reshape(r, B_HEADS, B_NOPE + B_V)
    wk = w[:, :, :B_NOPE].reshape(r, B_HEADS * B_NOPE)
    wv = w[:, :, B_NOPE:].reshape(r, B_HEADS * B_V)
    return wk.astype(BF16), wv.astype(BF16)


def kernel(x, norm_pre, norm_post, w_in, a_q_norm, a_k_norm, b_q_norm, b_kv_norm, b_w_uq, b_w_ukv, c_rpb, w_out):
    batch, seq, d = x.shape
    depth = w_in.shape[0]
    assert seq % C_TQ == 0 and seq // GRID_W >= C_WROWS and w_in.shape[2] == W_IN_COLS - 64
    tabs = _rope_tables(seq)
    h = x.reshape(batch * seq, d)
    for l in range(depth):
        wuk, wuv = _prep_w_ukv(b_w_ukv[l])
        qa, ka, va, qb, kb, vb, qc, kc, vc, gate = _inproj(
            h, seq, norm_pre[l][None], _prep_w_in(w_in[l]), _prep_w_uq(b_w_uq[l]), wuk, wuv,
            a_q_norm[l][None], a_k_norm[l][None], b_q_norm[l][None], b_kv_norm[l][None],
            tabs, tm=256)
        mix_a = _dense_attn(qa, ka, va, gate, seq, heads=A_HEADS, kv_heads=A_KV_HEADS,
                            dq=HEAD_DIM, dv=HEAD_DIM, gate_off=0, tq=256, name="attn_a")
        mix_b = _dense_attn(qb, kb, vb, gate, seq, heads=B_HEADS, kv_heads=B_HEADS,
                            dq=B_QK_PAD, dv=B_V, gate_off=A_Q, tq=256, name="attn_b")
        bias, pat_ids = _nbr_bias(c_rpb[l], seq)
        mix_c = _nbr_attn(qc, kc, vc, bias, pat_ids, gate, seq, A_Q + B_G)
        h = _outproj(mix_a, mix_b, mix_c, h, w_out[l].astype(BF16), norm_post[l][None], tm=512)
    return h.reshape(batch, seq, d)
```

```python
import functools

import jax
import jax.numpy as jnp
import numpy as np
from jax import lax
from jax.experimental import pallas as pl
from jax.experimental.pallas import tpu as pltpu

F32 = jnp.float32
BF16 = jnp.bfloat16

GRID_W = 64
HEAD_DIM = 128
A_HEADS = 8
A_KV_HEADS = 2
A_GROUP = A_HEADS // A_KV_HEADS
B_HEADS = 4
B_Q_LORA = 512
B_KV_LORA = 256
B_NOPE = 128
B_ROPE = 64
B_V = 128
B_QK_PAD = 256
C_HEADS = 4
WIN_ROWS = 8
WIN_COLS = 16
ROPE_THETA = 10000.0
NORM_EPS = 1e-6
NEG_BIG = -1e30
LOG2E = 1.4426950408889634

A_Q = A_HEADS * HEAD_DIM
A_KV = A_KV_HEADS * HEAD_DIM
B_G = B_HEADS * B_V
C_W = C_HEADS * HEAD_DIM
D_MIX = A_Q + B_G + C_W

OFF_AQ = 0
OFF_AK = OFF_AQ + A_Q
OFF_AV = OFF_AK + A_KV
OFF_AG = OFF_AV + A_KV
OFF_BCQ = OFF_AG + A_Q
OFF_BCKV = OFF_BCQ + B_Q_LORA
OFF_BG = OFF_BCKV + B_KV_LORA
OFF_CQ = OFF_BG + B_G
OFF_CK = OFF_CQ + C_W
OFF_CV = OFF_CK + C_W
OFF_CG = OFF_CV + C_W
OFF_BKR = OFF_CG + C_W
W_IN_COLS = OFF_BKR + 128

C_QROWS = 4
C_WROWS = 12
C_TQ = C_QROWS * GRID_W
C_TK = C_WROWS * GRID_W

VMEM_LIMIT_BYTES = 56 * 1024 * 1024


def _rms(x, gain):
    return x * lax.rsqrt(jnp.mean(x * x, axis=-1, keepdims=True) + NORM_EPS) * gain


def _rope(x, cos, sin_lo, sin_hi, half):
    up = pltpu.roll(x, 128 - half, 1)
    dn = pltpu.roll(x, half, 1)
    return x * cos + up * sin_lo + dn * sin_hi


def _silu(g):
    return g * (1.0 / (1.0 + jnp.exp(-g)))


def _inproj_kernel(x_ref, gpre_ref, w_ref, wuq_ref, wuk_ref, wuv_ref,
                   aqn_ref, akn_ref, bqn_ref, bkvn_ref,
                   cos_a, slo_a, shi_a, cos_b, slo_b, shi_b,
                   qa_ref, ka_ref, va_ref, qb_ref, kb_ref, vb_ref,
                   qc_ref, kc_ref, vc_ref, gate_ref, h_scr):
    x = x_ref[...]
    h_scr[...] = (x * gpre_ref[...]).astype(BF16)
    inv_rms = lax.rsqrt(jnp.mean(x * x, axis=-1, keepdims=True) + NORM_EPS)
    tm = x.shape[0]

    def proj(off, width):
        return jnp.dot(h_scr[...], w_ref[:, off:off + width], preferred_element_type=F32) * inv_rms

    def with_ones(ref, hd, v):
        ref[:, 2 * hd * HEAD_DIM:(2 * hd + 1) * HEAD_DIM] = v.astype(BF16)
        ref[:, (2 * hd + 1) * HEAD_DIM:(2 * hd + 2) * HEAD_DIM] = jnp.ones((tm, HEAD_DIM), BF16)

    ca, sla, sha = cos_a[...], slo_a[...], shi_a[...]
    cb, slb, shb = cos_b[...], slo_b[...], shi_b[...]
    scale_a = HEAD_DIM ** -0.5 * LOG2E
    scale_b = (B_NOPE + B_ROPE) ** -0.5 * LOG2E

    cq = _rms(proj(OFF_BCQ, B_Q_LORA), bqn_ref[...]).astype(BF16)
    ckv = _rms(proj(OFF_BCKV, B_KV_LORA), bkvn_ref[...]).astype(BF16)
    kpe = _rope(proj(OFF_BKR, 128), cb, slb, shb, 16).astype(BF16)

    pq = proj(OFF_AQ, A_Q)
    for hd in range(A_HEADS):
        sl = slice(hd * HEAD_DIM, (hd + 1) * HEAD_DIM)
        q = _rope(_rms(pq[:, sl], aqn_ref[...]), ca, sla, sha, 32)
        qa_ref[:, sl] = (q * scale_a).astype(BF16)
    pk = proj(OFF_AK, A_KV)
    pv = proj(OFF_AV, A_KV)
    for hd in range(A_KV_HEADS):
        sl = slice(hd * HEAD_DIM, (hd + 1) * HEAD_DIM)
        ka_ref[:, sl] = _rope(_rms(pk[:, sl], akn_ref[...]), ca, sla, sha, 32).astype(BF16)
        with_ones(va_ref, hd, pv[:, sl])
    gate_ref[:, 0:A_Q] = _silu(proj(OFF_AG, A_Q)).astype(BF16)

    qb = jnp.dot(cq, wuq_ref[...], preferred_element_type=F32)
    kn = jnp.dot(ckv, wuk_ref[...], preferred_element_type=F32)
    vb = jnp.dot(ckv, wuv_ref[...], preferred_element_type=F32)
    for hd in range(B_HEADS):
        lo = hd * B_QK_PAD
        qb_ref[:, lo:lo + B_NOPE] = (qb[:, lo:lo + B_NOPE] * scale_b).astype(BF16)
        qpe = _rope(qb[:, lo + B_NOPE:lo + B_QK_PAD], cb, slb, shb, 16)
        qb_ref[:, lo + B_NOPE:lo + B_QK_PAD] = (qpe * scale_b).astype(BF16)
        kb_ref[:, lo:lo + B_NOPE] = kn[:, hd * B_NOPE:(hd + 1) * B_NOPE].astype(BF16)
        kb_ref[:, lo + B_NOPE:lo + B_QK_PAD] = kpe
        with_ones(vb_ref, hd, vb[:, hd * B_V:(hd + 1) * B_V])
    gate_ref[:, A_Q:A_Q + B_G] = _silu(proj(OFF_BG, B_G)).astype(BF16)

    gate_ref[:, A_Q + B_G:D_MIX] = _silu(proj(OFF_CG, C_W)).astype(BF16)
    qc_ref[...] = (proj(OFF_CQ, C_W) * scale_a).astype(BF16)
    pvc = proj(OFF_CV, C_W)
    for hd in range(C_HEADS):
        with_ones(vc_ref, hd, pvc[:, hd * HEAD_DIM:(hd + 1) * HEAD_DIM])
    kc_ref[...] = proj(OFF_CK, C_W).astype(BF16)


def _inproj(x2d, seq, layer, gpre, w_all, wuq, wuk, wuv, aqn, akn, bqn, bkvn, tabs, tm):
    t, d = x2d.shape
    n_s = seq // tm
    row = lambda i: (i, 0)
    fix = lambda i: (0, 0)
    tab = lambda i: (i % n_s, 0)

    def resident(a):
        return pl.BlockSpec(a.shape, fix, pipeline_mode=pl.Buffered(1))

    w_spec = pl.BlockSpec((None,) + w_all.shape[1:], lambda i: (layer, 0, 0),
                          pipeline_mode=pl.Buffered(1))

    out_widths = (A_Q, A_KV, 2 * A_KV, B_HEADS * B_QK_PAD, B_HEADS * B_QK_PAD, 2 * B_G,
                  C_W, C_W, 2 * C_W, D_MIX)
    return pl.pallas_call(
        _inproj_kernel,
        grid=(t // tm,),
        in_specs=[pl.BlockSpec((tm, d), row), resident(gpre), w_spec, resident(wuq),
                  resident(wuk), resident(wuv), resident(aqn), resident(akn), resident(bqn),
                  resident(bkvn)] + [pl.BlockSpec((tm, 128), tab)] * 6,
        out_specs=[pl.BlockSpec((tm, n), row) for n in out_widths],
        out_shape=[jax.ShapeDtypeStruct((t, n), BF16) for n in out_widths],
        scratch_shapes=[pltpu.VMEM((tm, d), BF16)],
        compiler_params=pltpu.CompilerParams(dimension_semantics=("parallel",),
                                             vmem_limit_bytes=VMEM_LIMIT_BYTES),
        name="inproj",
    )(x2d, gpre, w_all, wuq, wuk, wuv, aqn, akn, bqn, bkvn, *tabs)


def _softmax_pv(s, v_ext, dv):
    m = jnp.max(s, axis=-1, keepdims=True)
    p = jnp.exp2(s - m).astype(BF16)
    oe = jnp.dot(p, v_ext, preferred_element_type=F32)
    return oe[:, :dv] * (1.0 / oe[:, dv:])


_NT = (((1,), (1,)), ((), ()))


def _dense_attn_kernel(q_ref, k_ref, v_ref, g_ref, o_ref, s_a, s_b, *, heads, shared_kv, dq, dv, tq):
    seq = q_ref.shape[0]

    def qk(hd, r, s_ref):
        kh = 0 if shared_kv else hd
        s_ref[...] = lax.dot_general(q_ref[pl.ds(r, tq), hd * dq:(hd + 1) * dq],
                                     k_ref[:, kh * dq:(kh + 1) * dq], _NT,
                                     preferred_element_type=F32)

    def finish(hd, r, s_ref):
        kh = 0 if shared_kv else hd
        o = _softmax_pv(s_ref[...], v_ref[:, kh * 2 * dv:(kh + 1) * 2 * dv], dv)
        g = g_ref[pl.ds(r, tq), hd * dv:(hd + 1) * dv].astype(F32)
        o_ref[pl.ds(r, tq), hd * dv:(hd + 1) * dv] = (o * g).astype(BF16)

    qk(0, 0, s_a)
    for hd in range(heads):

        def body(j, carry):
            r0 = pl.multiple_of(j * 2 * tq, 2 * tq)
            r1 = pl.multiple_of(r0 + tq, tq)
            qk(hd, r1, s_b)
            finish(hd, r0, s_a)
            qk(hd, pl.multiple_of(r0 + 2 * tq, 2 * tq), s_a)
            finish(hd, r1, s_b)
            return carry

        lax.fori_loop(0, seq // (2 * tq) - 1, body, 0)
        r0 = seq - 2 * tq
        qk(hd, r0 + tq, s_b)
        finish(hd, r0, s_a)
        if hd + 1 < heads:
            qk(hd + 1, 0, s_a)
        finish(hd, r0 + tq, s_b)


def _dense_attn(q, k, v_ext, gate, seq, *, heads, kv_heads, dq, dv, gate_off, tq, name):
    t = q.shape[0]
    shared_kv = kv_heads < heads
    per_step = heads // kv_heads if shared_kv else heads
    n_steps = heads // per_step
    kv_per_step = 1 if shared_kv else per_step
    g_blk = gate_off // (per_step * dv)
    kern = functools.partial(_dense_attn_kernel, heads=per_step, shared_kv=shared_kv,
                             dq=dq, dv=dv, tq=tq)
    return pl.pallas_call(
        kern,
        grid=(t // seq, n_steps),
        in_specs=[pl.BlockSpec((seq, per_step * dq), lambda b, h: (b, h)),
                  pl.BlockSpec((seq, kv_per_step * dq), lambda b, h: (b, h)),
                  pl.BlockSpec((seq, kv_per_step * 2 * dv), lambda b, h: (b, h)),
                  pl.BlockSpec((seq, per_step * dv), lambda b, h: (b, g_blk + h))],
        out_specs=pl.BlockSpec((seq, per_step * dv), lambda b, h: (b, h)),
        out_shape=jax.ShapeDtypeStruct((t, heads * dv), BF16),
        scratch_shapes=[pltpu.VMEM((tq, seq), F32), pltpu.VMEM((tq, seq), F32)],
        compiler_params=pltpu.CompilerParams(dimension_semantics=("parallel", "parallel"),
                                             vmem_limit_bytes=VMEM_LIMIT_BYTES),
        name=name,
    )(q, k, v_ext, gate)


def _nbr_attn_kernel(pid_ref, q_ref, k_ref, v_ref, bias_ref, g_ref, o_ref, *, max_ws):
    del pid_ref
    i = pl.program_id(1)
    ws = jnp.clip(C_QROWS * i - WIN_ROWS // 2, 0, max_ws)
    start = pl.multiple_of(ws * GRID_W, GRID_W)
    for hd in range(C_HEADS):
        sl = slice(hd * HEAD_DIM, (hd + 1) * HEAD_DIM)
        kw = k_ref[pl.ds(start, C_TK), sl]
        vw = v_ref[pl.ds(start, C_TK), 2 * hd * HEAD_DIM:2 * (hd + 1) * HEAD_DIM]
        s = lax.dot_general(q_ref[:, sl], kw, _NT, preferred_element_type=F32) + bias_ref[hd]
        o = _softmax_pv(s, vw, HEAD_DIM) * g_ref[:, sl].astype(F32)
        o_ref[:, sl] = o.astype(BF16)


def _nbr_attn(q, k, v, bias, pat_ids, layer, gate, seq, gate_off):
    t = q.shape[0]
    nb = seq // C_TQ
    g_blk = gate_off // C_W
    kern = functools.partial(_nbr_attn_kernel, max_ws=seq // GRID_W - C_WROWS)
    grid_spec = pltpu.PrefetchScalarGridSpec(
        num_scalar_prefetch=1,
        grid=(t // seq, nb),
        in_specs=[pl.BlockSpec((C_TQ, C_W), lambda b, i, pid: (b * nb + i, 0)),
                  pl.BlockSpec((seq, C_W), lambda b, i, pid: (b, 0)),
                  pl.BlockSpec((seq, 2 * C_W), lambda b, i, pid: (b, 0)),
                  pl.BlockSpec((None, None, C_HEADS, C_TQ, C_TK),
                               lambda b, i, pid: (pid[i], layer, 0, 0, 0)),
                  pl.BlockSpec((C_TQ, C_W), lambda b, i, pid: (b * nb + i, g_blk))],
        out_specs=pl.BlockSpec((C_TQ, C_W), lambda b, i, pid: (b * nb + i, 0)))
    return pl.pallas_call(
        kern,
        grid_spec=grid_spec,
        out_shape=jax.ShapeDtypeStruct((t, C_W), BF16),
        compiler_params=pltpu.CompilerParams(dimension_semantics=("parallel", "arbitrary"),
                                             vmem_limit_bytes=VMEM_LIMIT_BYTES),
        name="attn_c",
    )(pat_ids, q, k, v, bias, gate)


def _nbr_bias(rpb, seq):
    depth, heads, n_dr, n_dc = rpb.shape
    g = depth * heads
    rows = seq // GRID_W
    nb = seq // C_TQ
    neg = NEG_BIG * LOG2E
    lead = GRID_W - WIN_COLS
    u = jnp.pad(rpb.reshape(g, n_dr, n_dc) * LOG2E, ((0, 0), (0, 0), (lead, 2 * GRID_W - n_dc - lead)))
    skew = jnp.tile(u, (1, 1, GRID_W))[:, :, GRID_W - 1:GRID_W - 1 + GRID_W * (2 * GRID_W - 1)]
    vals = skew.reshape(g, n_dr, GRID_W, 2 * GRID_W - 1)[..., :GRID_W]
    qc = np.arange(GRID_W)[:, None]
    kc = np.arange(GRID_W)[None, :]
    c0 = np.clip(qc - WIN_COLS // 2, 0, GRID_W - WIN_COLS)
    valid_c = (kc >= c0) & (kc < c0 + WIN_COLS)
    slabs = jnp.where(valid_c[None, None], vals, neg)
    blk = np.arange(nb)[:, None, None]
    r = C_QROWS * blk + np.arange(C_QROWS)[None, :, None]
    ws = np.clip(C_QROWS * blk - WIN_ROWS // 2, 0, rows - C_WROWS)
    kr = ws + np.arange(C_WROWS)[None, None, :]
    r0 = np.clip(r - WIN_ROWS // 2, 0, rows - WIN_ROWS)
    valid_r = (kr >= r0) & (kr < r0 + WIN_ROWS)
    slab_id = np.where(valid_r, kr - r + (WIN_ROWS - 1), -1)
    pats, pat_ids = np.unique(slab_id.reshape(nb, -1), axis=0, return_inverse=True)
    pats = pats.reshape(-1, C_QROWS, C_WROWS)
    front, span = C_QROWS + C_WROWS, C_QROWS + C_WROWS - 1
    padded = jnp.pad(slabs, ((0, 0), (front, front), (0, 0), (0, 0)), constant_values=neg)
    rel = np.arange(C_WROWS)[None, :] - np.arange(C_QROWS)[:, None]
    out = []
    for pat in pats:
        valid = pat >= 0
        off = int((pat - rel)[valid][0])
        assert ((pat - rel)[valid] == off).all(), "slab index must be Toeplitz in (query row, window row)"
        lo = front + off - (C_QROWS - 1)
        w = padded[:, lo:lo + span + 1].reshape(g, (span + 1) * GRID_W * GRID_W)
        sk = jnp.tile(w, (1, C_QROWS))[:, (C_QROWS - 1) * GRID_W * GRID_W:
                                          (C_QROWS - 1 + C_QROWS * span) * GRID_W * GRID_W]
        sk = sk.reshape(g, C_QROWS, span, GRID_W, GRID_W)[:, :, :C_WROWS]
        sk = jnp.where(valid[None, :, :, None, None], sk, neg)
        out.append(jnp.transpose(sk, (0, 1, 3, 2, 4)).reshape(depth, heads, C_TQ, C_TK))
    return jnp.stack(out, axis=0), jnp.asarray(pat_ids.reshape(nb), jnp.int32)


def _outproj_kernel(ma_ref, mb_ref, mc_ref, x_ref, w_ref, gpost_ref, o_ref):
    y = jnp.dot(ma_ref[...], w_ref[0:A_Q, :], preferred_element_type=F32)
    y += jnp.dot(mb_ref[...], w_ref[A_Q:A_Q + B_G, :], preferred_element_type=F32)
    y += jnp.dot(mc_ref[...], w_ref[A_Q + B_G:D_MIX, :], preferred_element_type=F32)
    o_ref[...] = x_ref[...] + _rms(y, gpost_ref[...])


def _outproj(ma, mb, mc, x2d, w, gpost, tm):
    t, d = x2d.shape
    row = lambda i: (i, 0)
    fix = lambda i: (0, 0)
    return pl.pallas_call(
        _outproj_kernel,
        grid=(t // tm,),
        in_specs=[pl.BlockSpec((tm, A_Q), row), pl.BlockSpec((tm, B_G), row),
                  pl.BlockSpec((tm, C_W), row), pl.BlockSpec((tm, d), row),
                  pl.BlockSpec(w.shape, fix, pipeline_mode=pl.Buffered(1)),
                  pl.BlockSpec(gpost.shape, fix, pipeline_mode=pl.Buffered(1))],
        out_specs=pl.BlockSpec((tm, d), row),
        out_shape=jax.ShapeDtypeStruct((t, d), F32),
        compiler_params=pltpu.CompilerParams(dimension_semantics=("parallel",),
                                             vmem_limit_bytes=VMEM_LIMIT_BYTES),
        name="outproj",
    )(ma, mb, mc, x2d, w, gpost)


def _rope_tables(seq):
    t = np.arange(seq)
    pos = np.stack([t // GRID_W, t % GRID_W], axis=0).astype(np.float64)

    def tables(n, lanes_used):
        j = np.arange(128)
        axis = np.minimum(j // n, 1)
        i = j % (n // 2)
        inv_freq = 1.0 / (ROPE_THETA ** (np.arange(0, n, 2, dtype=np.float64) / n))
        ang = pos[axis, :].T * inv_freq[i][None, :]
        used = (j < lanes_used)[None, :]
        low = ((j % n) < n // 2)[None, :]
        cos = np.where(used, np.cos(ang), 0.0)
        sin = np.where(used, np.sin(ang), 0.0)
        return tuple(jnp.asarray(a, F32) for a in (cos, np.where(low, -sin, 0.0), np.where(low, 0.0, sin)))

    return tables(HEAD_DIM // 2, 128) + tables(B_ROPE // 2, B_ROPE)


def _w_in_repack_kernel(w_ref, o_ref):
    kr_lo = OFF_BG
    kr_hi = kr_lo + B_ROPE
    rows = w_ref.shape[0]
    o_ref[:, 0:kr_lo] = w_ref[:, 0:kr_lo].astype(BF16)
    o_ref[:, kr_lo:OFF_BKR] = w_ref[:, kr_hi:].astype(BF16)
    tail = jnp.concatenate([w_ref[:, kr_lo:kr_hi], jnp.zeros((rows, 128 - B_ROPE), F32)], axis=1)
    o_ref[:, OFF_BKR:W_IN_COLS] = tail.astype(BF16)


def _prep_w_in(w, tr=256):
    depth, d, n = w.shape
    return pl.pallas_call(
        _w_in_repack_kernel,
        grid=(depth, d // tr),
        in_specs=[pl.BlockSpec((None, tr, n), lambda l, i: (l, i, 0))],
        out_specs=pl.BlockSpec((None, tr, W_IN_COLS), lambda l, i: (l, i, 0)),
        out_shape=jax.ShapeDtypeStruct((depth, d, W_IN_COLS), BF16),
        compiler_params=pltpu.CompilerParams(dimension_semantics=("parallel", "parallel"),
                                             vmem_limit_bytes=VMEM_LIMIT_BYTES),
        name="w_in_repack",
    )(w)


def _prep_w_uq(w):
    r = w.shape[0]
    w = w.reshape(r, B_HEADS, B_NOPE + B_ROPE)
    w = jnp.pad(w, ((0, 0), (0, 0), (0, B_QK_PAD - B_NOPE - B_ROPE)))
    return w.reshape(r, B_HEADS * B_QK_PAD).astype(BF16)


def _prep_w_ukv(w):
    r = w.shape[0]
    w = w.reshape(r, B_HEADS, B_NOPE + B_V)
    wk = w[:, :, :B_NOPE].reshape(r, B_HEADS * B_NOPE)
    wv = w[:, :, B_NOPE:].reshape(r, B_HEADS * B_V)
    return wk.astype(BF16), wv.astype(BF16)


def kernel(x, norm_pre, norm_post, w_in, a_q_norm, a_k_norm, b_q_norm, b_kv_norm, b_w_uq, b_w_ukv, c_rpb, w_out):
    batch, seq, d = x.shape
    depth = w_in.shape[0]
    assert seq % C_TQ == 0 and seq // GRID_W >= C_WROWS and w_in.shape[2] == W_IN_COLS - 64
    tabs = _rope_tables(seq)
    bias, pat_ids = _nbr_bias(c_rpb, seq)
    w_in_all = _prep_w_in(w_in)
    h = x.reshape(batch * seq, d)
    for l in range(depth):
        wuk, wuv = _prep_w_ukv(b_w_ukv[l])
        qa, ka, va, qb, kb, vb, qc, kc, vc, gate = _inproj(
            h, seq, l, norm_pre[l][None], w_in_all, _prep_w_uq(b_w_uq[l]), wuk, wuv,
            a_q_norm[l][None], a_k_norm[l][None], b_q_norm[l][None], b_kv_norm[l][None],
            tabs, tm=256)
        mix_a = _dense_attn(qa, ka, va, gate, seq, heads=A_HEADS, kv_heads=A_KV_HEADS,
                            dq=HEAD_DIM, dv=HEAD_DIM, gate_off=0, tq=256, name="attn_a")
        mix_b = _dense_attn(qb, kb, vb, gate, seq, heads=B_HEADS, kv_heads=B_HEADS,
                            dq=B_QK_PAD, dv=B_V, gate_off=A_Q, tq=256, name="attn_b")
        mix_c = _nbr_attn(qc, kc, vc, bias, pat_ids, l, gate, seq, A_Q + B_G)
        h = _outproj(mix_a, mix_b, mix_c, h, w_out[l].astype(BF16), norm_post[l][None], tm=512)
    return h.reshape(batch, seq, d)
```

```python
import functools

import jax
import jax.numpy as jnp
import numpy as np
from jax import lax
from jax.experimental import pallas as pl
from jax.experimental.pallas import tpu as pltpu

F32 = jnp.float32
BF16 = jnp.bfloat16

GRID_W = 64
HEAD_DIM = 128
A_HEADS = 8
A_KV_HEADS = 2
A_GROUP = A_HEADS // A_KV_HEADS
B_HEADS = 4
B_Q_LORA = 512
B_KV_LORA = 256
B_NOPE = 128
B_ROPE = 64
B_V = 128
B_QK_PAD = 256
C_HEADS = 4
WIN_ROWS = 8
WIN_COLS = 16
ROPE_THETA = 10000.0
NORM_EPS = 1e-6
NEG_BIG = -1e30
LOG2E = 1.4426950408889634

A_Q = A_HEADS * HEAD_DIM
A_KV = A_KV_HEADS * HEAD_DIM
B_G = B_HEADS * B_V
C_W = C_HEADS * HEAD_DIM
D_MIX = A_Q + B_G + C_W

OFF_AQ = 0
OFF_AK = OFF_AQ + A_Q
OFF_AV = OFF_AK + A_KV
OFF_AG = OFF_AV + A_KV
OFF_BCQ = OFF_AG + A_Q
OFF_BCKV = OFF_BCQ + B_Q_LORA
OFF_BG = OFF_BCKV + B_KV_LORA
OFF_CQ = OFF_BG + B_G
OFF_CK = OFF_CQ + C_W
OFF_CV = OFF_CK + C_W
OFF_CG = OFF_CV + C_W
OFF_BKR = OFF_CG + C_W
W_IN_COLS = OFF_BKR + 128

C_QROWS = 4
C_WROWS = 12
C_TQ = C_QROWS * GRID_W
C_TK = C_WROWS * GRID_W

VMEM_LIMIT_BYTES = 56 * 1024 * 1024


def _rms(x, gain):
    return x * lax.rsqrt(jnp.mean(x * x, axis=-1, keepdims=True) + NORM_EPS) * gain


def _rope(x, cos, sin_lo, sin_hi, half):
    up = pltpu.roll(x, 128 - half, 1)
    dn = pltpu.roll(x, half, 1)
    return x * cos + up * sin_lo + dn * sin_hi


def _silu(g):
    return g * (1.0 / (1.0 + jnp.exp(-g)))


_NT = (((1,), (1,)), ((), ()))


def _inproj_kernel(x_ref, gpre_ref, w_ref, wuq_ref, wuk_ref, wuv_ref,
                   aqn_ref, akn_ref, bqn_ref, bkvn_ref,
                   cos_a, slo_a, shi_a, cos_b, slo_b, shi_b,
                   qa_ref, ka_ref, va_ref, qb_ref, kb_ref, vb_ref,
                   qc_ref, kc_ref, vc_ref, gate_ref, h_scr):
    x = x_ref[...]
    h_scr[...] = (x * gpre_ref[...]).astype(BF16)
    inv_rms = lax.rsqrt(jnp.mean(x * x, axis=-1, keepdims=True) + NORM_EPS)
    tm = x.shape[0]

    def proj(off, width):
        return jnp.dot(h_scr[...], w_ref[:, off:off + width], preferred_element_type=F32) * inv_rms

    def with_ones(ref, hd, v):
        ref[:, 2 * hd * HEAD_DIM:(2 * hd + 1) * HEAD_DIM] = v.astype(BF16)
        ref[:, (2 * hd + 1) * HEAD_DIM:(2 * hd + 2) * HEAD_DIM] = jnp.ones((tm, HEAD_DIM), BF16)

    ca, sla, sha = cos_a[...], slo_a[...], shi_a[...]
    cb, slb, shb = cos_b[...], slo_b[...], shi_b[...]
    scale_a = HEAD_DIM ** -0.5 * LOG2E
    scale_b = (B_NOPE + B_ROPE) ** -0.5 * LOG2E

    cq = _rms(proj(OFF_BCQ, B_Q_LORA), bqn_ref[...]).astype(BF16)
    ckv = _rms(proj(OFF_BCKV, B_KV_LORA), bkvn_ref[...]).astype(BF16)
    kpe = _rope(proj(OFF_BKR, 128), cb, slb, shb, 16).astype(BF16)

    pq = proj(OFF_AQ, A_Q)
    for hd in range(A_HEADS):
        sl = slice(hd * HEAD_DIM, (hd + 1) * HEAD_DIM)
        q = _rope(_rms(pq[:, sl], aqn_ref[...]), ca, sla, sha, 32)
        qa_ref[:, sl] = (q * scale_a).astype(BF16)
    pk = proj(OFF_AK, A_KV)
    pv = proj(OFF_AV, A_KV)
    for hd in range(A_KV_HEADS):
        sl = slice(hd * HEAD_DIM, (hd + 1) * HEAD_DIM)
        ka_ref[:, sl] = _rope(_rms(pk[:, sl], akn_ref[...]), ca, sla, sha, 32).astype(BF16)
        with_ones(va_ref, hd, pv[:, sl])
    gate_ref[:, 0:A_Q] = _silu(proj(OFF_AG, A_Q)).astype(BF16)

    qb = jnp.dot(cq, wuq_ref[...], preferred_element_type=F32)
    kn = jnp.dot(ckv, wuk_ref[...], preferred_element_type=F32)
    vb = jnp.dot(ckv, wuv_ref[...], preferred_element_type=F32)
    for hd in range(B_HEADS):
        lo = hd * B_QK_PAD
        qb_ref[:, lo:lo + B_NOPE] = (qb[:, lo:lo + B_NOPE] * scale_b).astype(BF16)
        qpe = _rope(qb[:, lo + B_NOPE:lo + B_QK_PAD], cb, slb, shb, 16)
        qb_ref[:, lo + B_NOPE:lo + B_QK_PAD] = (qpe * scale_b).astype(BF16)
        kb_ref[:, lo:lo + B_NOPE] = kn[:, hd * B_NOPE:(hd + 1) * B_NOPE].astype(BF16)
        kb_ref[:, lo + B_NOPE:lo + B_QK_PAD] = kpe
        with_ones(vb_ref, hd, vb[:, hd * B_V:(hd + 1) * B_V])
    gate_ref[:, A_Q:A_Q + B_G] = _silu(proj(OFF_BG, B_G)).astype(BF16)

    gate_ref[:, A_Q + B_G:D_MIX] = _silu(proj(OFF_CG, C_W)).astype(BF16)
    qc_ref[...] = (proj(OFF_CQ, C_W) * scale_a).astype(BF16)
    pvc = proj(OFF_CV, C_W)
    for hd in range(C_HEADS):
        with_ones(vc_ref, hd, pvc[:, hd * HEAD_DIM:(hd + 1) * HEAD_DIM])
    kc_ref[...] = proj(OFF_CK, C_W).astype(BF16)


def _inproj(x2d, seq, layer, gpre, w_all, wuq, wuk, wuv, aqn, akn, bqn, bkvn, tabs, tm):
    t, d = x2d.shape
    n_s = seq // tm
    row = lambda i: (i, 0)
    fix = lambda i: (0, 0)
    tab = lambda i: (i % n_s, 0)

    def resident(a):
        return pl.BlockSpec(a.shape, fix, pipeline_mode=pl.Buffered(1))

    w_spec = pl.BlockSpec((None,) + w_all.shape[1:], lambda i: (layer, 0, 0),
                          pipeline_mode=pl.Buffered(1))

    out_widths = (A_Q, A_KV, 2 * A_KV, B_HEADS * B_QK_PAD, B_HEADS * B_QK_PAD, 2 * B_G,
                  C_W, C_W, 2 * C_W, D_MIX)
    return pl.pallas_call(
        _inproj_kernel,
        grid=(t // tm,),
        in_specs=[pl.BlockSpec((tm, d), row), resident(gpre), w_spec, resident(wuq),
                  resident(wuk), resident(wuv), resident(aqn), resident(akn), resident(bqn),
                  resident(bkvn)] + [pl.BlockSpec((tm, 128), tab)] * 6,
        out_specs=[pl.BlockSpec((tm, n), row) for n in out_widths],
        out_shape=[jax.ShapeDtypeStruct((t, n), BF16) for n in out_widths],
        scratch_shapes=[pltpu.VMEM((tm, d), BF16)],
        compiler_params=pltpu.CompilerParams(dimension_semantics=("parallel",),
                                             vmem_limit_bytes=VMEM_LIMIT_BYTES),
        name="inproj",
    )(x2d, gpre, w_all, wuq, wuk, wuv, aqn, akn, bqn, bkvn, *tabs)


def _softmax_pv(s, v_ext, dv):
    m = jnp.max(s, axis=-1, keepdims=True)
    p = jnp.exp2(s - m).astype(BF16)
    oe = jnp.dot(p, v_ext, preferred_element_type=F32)
    return oe[:, :dv] * (1.0 / oe[:, dv:])


def _dense_attn_kernel(q_ref, k_ref, v_ref, g_ref, o_ref, s_a, s_b, *, heads, shared_kv, dq, dv, tq):
    seq = q_ref.shape[0]

    def qk(hd, r, s_ref):
        kh = 0 if shared_kv else hd
        s_ref[...] = lax.dot_general(q_ref[pl.ds(r, tq), hd * dq:(hd + 1) * dq],
                                     k_ref[:, kh * dq:(kh + 1) * dq], _NT,
                                     preferred_element_type=F32)

    def finish(hd, r, s_ref):
        kh = 0 if shared_kv else hd
        o = _softmax_pv(s_ref[...], v_ref[:, kh * 2 * dv:(kh + 1) * 2 * dv], dv)
        g = g_ref[pl.ds(r, tq), hd * dv:(hd + 1) * dv].astype(F32)
        o_ref[pl.ds(r, tq), hd * dv:(hd + 1) * dv] = (o * g).astype(BF16)

    qk(0, 0, s_a)
    for hd in range(heads):

        def body(j, carry):
            r0 = pl.multiple_of(j * 2 * tq, 2 * tq)
            r1 = pl.multiple_of(r0 + tq, tq)
            qk(hd, r1, s_b)
            finish(hd, r0, s_a)
            qk(hd, pl.multiple_of(r0 + 2 * tq, 2 * tq), s_a)
            finish(hd, r1, s_b)
            return carry

        lax.fori_loop(0, seq // (2 * tq) - 1, body, 0)
        r0 = seq - 2 * tq
        qk(hd, r0 + tq, s_b)
        finish(hd, r0, s_a)
        if hd + 1 < heads:
            qk(hd + 1, 0, s_a)
        finish(hd, r0 + tq, s_b)


def _dense_attn(q, k, v_ext, gate, seq, *, heads, kv_heads, dq, dv, gate_off, tq, name):
    t = q.shape[0]
    shared_kv = kv_heads < heads
    per_step = heads // kv_heads if shared_kv else heads
    n_steps = heads // per_step
    kv_per_step = 1 if shared_kv else per_step
    g_blk = gate_off // (per_step * dv)
    kern = functools.partial(_dense_attn_kernel, heads=per_step, shared_kv=shared_kv,
                             dq=dq, dv=dv, tq=tq)
    return pl.pallas_call(
        kern,
        grid=(t // seq, n_steps),
        in_specs=[pl.BlockSpec((seq, per_step * dq), lambda b, h: (b, h)),
                  pl.BlockSpec((seq, kv_per_step * dq), lambda b, h: (b, h)),
                  pl.BlockSpec((seq, kv_per_step * 2 * dv), lambda b, h: (b, h)),
                  pl.BlockSpec((seq, per_step * dv), lambda b, h: (b, g_blk + h))],
        out_specs=pl.BlockSpec((seq, per_step * dv), lambda b, h: (b, h)),
        out_shape=jax.ShapeDtypeStruct((t, heads * dv), BF16),
        scratch_shapes=[pltpu.VMEM((tq, seq), F32), pltpu.VMEM((tq, seq), F32)],
        compiler_params=pltpu.CompilerParams(dimension_semantics=("parallel", "parallel"),
                                             vmem_limit_bytes=VMEM_LIMIT_BYTES),
        name=name,
    )(q, k, v_ext, gate)


def _nbr_attn_kernel(pid_ref, q_ref, k_ref, v_ref, bias_ref, g_ref, o_ref, *, max_ws):
    del pid_ref
    i = pl.program_id(1)
    ws = jnp.clip(C_QROWS * i - WIN_ROWS // 2, 0, max_ws)
    start = pl.multiple_of(ws * GRID_W, GRID_W)
    for hd in range(C_HEADS):
        sl = slice(hd * HEAD_DIM, (hd + 1) * HEAD_DIM)
        kw = k_ref[pl.ds(start, C_TK), sl]
        vw = v_ref[pl.ds(start, C_TK), 2 * hd * HEAD_DIM:2 * (hd + 1) * HEAD_DIM]
        s = lax.dot_general(q_ref[:, sl], kw, _NT, preferred_element_type=F32) + bias_ref[hd]
        o = _softmax_pv(s, vw, HEAD_DIM) * g_ref[:, sl].astype(F32)
        o_ref[:, sl] = o.astype(BF16)


def _nbr_attn(q, k, v, bias, pat_ids, layer, gate, seq, gate_off):
    t = q.shape[0]
    nb = seq // C_TQ
    g_blk = gate_off // C_W
    kern = functools.partial(_nbr_attn_kernel, max_ws=seq // GRID_W - C_WROWS)
    grid_spec = pltpu.PrefetchScalarGridSpec(
        num_scalar_prefetch=1,
        grid=(t // seq, nb),
        in_specs=[pl.BlockSpec((C_TQ, C_W), lambda b, i, pid: (b * nb + i, 0)),
                  pl.BlockSpec((seq, C_W), lambda b, i, pid: (b, 0)),
                  pl.BlockSpec((seq, 2 * C_W), lambda b, i, pid: (b, 0)),
                  pl.BlockSpec((None, None, C_HEADS, C_TQ, C_TK),
                               lambda b, i, pid: (pid[i], layer, 0, 0, 0)),
                  pl.BlockSpec((C_TQ, C_W), lambda b, i, pid: (b * nb + i, g_blk))],
        out_specs=pl.BlockSpec((C_TQ, C_W), lambda b, i, pid: (b * nb + i, 0)))
    return pl.pallas_call(
        kern,
        grid_spec=grid_spec,
        out_shape=jax.ShapeDtypeStruct((t, C_W), BF16),
        compiler_params=pltpu.CompilerParams(dimension_semantics=("parallel", "arbitrary"),
                                             vmem_limit_bytes=VMEM_LIMIT_BYTES),
        name="attn_c",
    )(pat_ids, q, k, v, bias, gate)


def _nbr_bias(rpb, seq):
    depth, heads, n_dr, n_dc = rpb.shape
    g = depth * heads
    rows = seq // GRID_W
    nb = seq // C_TQ
    neg = NEG_BIG * LOG2E
    lead = GRID_W - WIN_COLS
    u = jnp.pad(rpb.reshape(g, n_dr, n_dc) * LOG2E, ((0, 0), (0, 0), (lead, 2 * GRID_W - n_dc - lead)))
    skew = jnp.tile(u, (1, 1, GRID_W))[:, :, GRID_W - 1:GRID_W - 1 + GRID_W * (2 * GRID_W - 1)]
    vals = skew.reshape(g, n_dr, GRID_W, 2 * GRID_W - 1)[..., :GRID_W]
    qc = np.arange(GRID_W)[:, None]
    kc = np.arange(GRID_W)[None, :]
    c0 = np.clip(qc - WIN_COLS // 2, 0, GRID_W - WIN_COLS)
    valid_c = (kc >= c0) & (kc < c0 + WIN_COLS)
    slabs = jnp.where(valid_c[None, None], vals, neg)
    blk = np.arange(nb)[:, None, None]
    r = C_QROWS * blk + np.arange(C_QROWS)[None, :, None]
    ws = np.clip(C_QROWS * blk - WIN_ROWS // 2, 0, rows - C_WROWS)
    kr = ws + np.arange(C_WROWS)[None, None, :]
    r0 = np.clip(r - WIN_ROWS // 2, 0, rows - WIN_ROWS)
    valid_r = (kr >= r0) & (kr < r0 + WIN_ROWS)
    slab_id = np.where(valid_r, kr - r + (WIN_ROWS - 1), -1)
    pats, pat_ids = np.unique(slab_id.reshape(nb, -1), axis=0, return_inverse=True)
    pats = np.where(pats < 0, n_dr, pats).reshape(-1, C_QROWS, C_WROWS)
    n_pat = pats.shape[0]
    slabs = jnp.concatenate([slabs, jnp.full((g, 1, GRID_W, GRID_W), neg, F32)], axis=1)
    pair_src = jnp.concatenate([slabs, slabs], axis=-1)
    table = tuple(tuple(tuple(int(s) for s in row) for row in pat) for pat in pats)
    bias = pl.pallas_call(
        functools.partial(_nbr_bias_kernel, table=table),
        grid=(g,),
        in_specs=[pl.BlockSpec((None, n_dr + 1, GRID_W, 2 * GRID_W), lambda i: (i, 0, 0, 0))],
        out_specs=pl.BlockSpec((n_pat, None, C_TQ, C_TK), lambda i: (0, i, 0, 0)),
        out_shape=jax.ShapeDtypeStruct((n_pat, g, C_TQ, C_TK), F32),
        compiler_params=pltpu.CompilerParams(dimension_semantics=("parallel",)),
        name="nbr_bias",
    )(pair_src)
    return (bias.reshape(n_pat, depth, heads, C_TQ, C_TK),
            jnp.asarray(pat_ids.reshape(nb), jnp.int32))


def _nbr_bias_kernel(src_ref, o_ref, *, table):
    left = lax.broadcasted_iota(jnp.int32, (GRID_W, 2 * GRID_W), 1) < GRID_W
    for p, pat in enumerate(table):
        for q, row in enumerate(pat):
            for kp in range(len(row) // 2):
                tile = jnp.where(left, src_ref[row[2 * kp]], src_ref[row[2 * kp + 1]])
                o_ref[p, q * GRID_W:(q + 1) * GRID_W, 2 * kp * GRID_W:2 * (kp + 1) * GRID_W] = tile


def _outproj_kernel(ma_ref, mb_ref, mc_ref, x_ref, w_ref, gpost_ref, o_ref):
    y = jnp.dot(ma_ref[...], w_ref[0:A_Q, :], preferred_element_type=F32)
    y += jnp.dot(mb_ref[...], w_ref[A_Q:A_Q + B_G, :], preferred_element_type=F32)
    y += jnp.dot(mc_ref[...], w_ref[A_Q + B_G:D_MIX, :], preferred_element_type=F32)
    o_ref[...] = x_ref[...] + _rms(y, gpost_ref[...])


def _outproj(ma, mb, mc, x2d, w, gpost, tm):
    t, d = x2d.shape
    row = lambda i: (i, 0)
    fix = lambda i: (0, 0)
    return pl.pallas_call(
        _outproj_kernel,
        grid=(t // tm,),
        in_specs=[pl.BlockSpec((tm, A_Q), row), pl.BlockSpec((tm, B_G), row),
                  pl.BlockSpec((tm, C_W), row), pl.BlockSpec((tm, d), row),
                  pl.BlockSpec(w.shape, fix, pipeline_mode=pl.Buffered(1)),
                  pl.BlockSpec(gpost.shape, fix, pipeline_mode=pl.Buffered(1))],
        out_specs=pl.BlockSpec((tm, d), row),
        out_shape=jax.ShapeDtypeStruct((t, d), F32),
        compiler_params=pltpu.CompilerParams(dimension_semantics=("parallel",),
                                             vmem_limit_bytes=VMEM_LIMIT_BYTES),
        name="outproj",
    )(ma, mb, mc, x2d, w, gpost)


def _rope_tables(seq):
    t = np.arange(seq)
    pos = np.stack([t // GRID_W, t % GRID_W], axis=0).astype(np.float64)

    def tables(n, lanes_used):
        j = np.arange(128)
        axis = np.minimum(j // n, 1)
        i = j % (n // 2)
        inv_freq = 1.0 / (ROPE_THETA ** (np.arange(0, n, 2, dtype=np.float64) / n))
        ang = pos[axis, :].T * inv_freq[i][None, :]
        used = (j < lanes_used)[None, :]
        low = ((j % n) < n // 2)[None, :]
        cos = np.where(used, np.cos(ang), 0.0)
        sin = np.where(used, np.sin(ang), 0.0)
        return tuple(jnp.asarray(a, F32) for a in (cos, np.where(low, -sin, 0.0), np.where(low, 0.0, sin)))

    return tables(HEAD_DIM // 2, 128) + tables(B_ROPE // 2, B_ROPE)


def _w_in_repack_kernel(w_ref, o_ref):
    kr_lo = OFF_BG
    kr_hi = kr_lo + B_ROPE
    tc = w_ref.shape[1]
    chunk = 256
    for lo in range(0, kr_lo, chunk):
        o_ref[:, lo:lo + chunk] = w_ref[lo:lo + chunk, :].T.astype(BF16)
    for lo in range(kr_lo, OFF_BKR, chunk):
        o_ref[:, lo:lo + chunk] = w_ref[lo + B_ROPE:lo + B_ROPE + chunk, :].T.astype(BF16)
    tail = jnp.concatenate([w_ref[kr_lo:kr_hi, :], jnp.zeros((128 - B_ROPE, tc), F32)], axis=0)
    o_ref[:, OFF_BKR:W_IN_COLS] = tail.T.astype(BF16)


def _prep_w_in(w, tc=256):
    depth, d, n = w.shape
    wt = jnp.swapaxes(w, 1, 2)
    return pl.pallas_call(
        _w_in_repack_kernel,
        grid=(depth, d // tc),
        in_specs=[pl.BlockSpec((None, n, tc), lambda l, i: (l, 0, i))],
        out_specs=pl.BlockSpec((None, tc, W_IN_COLS), lambda l, i: (l, i, 0)),
        out_shape=jax.ShapeDtypeStruct((depth, d, W_IN_COLS), BF16),
        compiler_params=pltpu.CompilerParams(dimension_semantics=("parallel", "parallel"),
                                             vmem_limit_bytes=VMEM_LIMIT_BYTES),
        name="w_in_repack",
    )(wt)


def _prep_w_uq(w):
    r = w.shape[0]
    w = w.reshape(r, B_HEADS, B_NOPE + B_ROPE)
    w = jnp.pad(w, ((0, 0), (0, 0), (0, B_QK_PAD - B_NOPE - B_ROPE)))
    return w.reshape(r, B_HEADS * B_QK_PAD).astype(BF16)


def _prep_w_ukv(w):
    r = w.shape[0]
    w = w.reshape(r, B_HEADS, B_NOPE + B_V)
    wk = w[:, :, :B_NOPE].reshape(r, B_HEADS * B_NOPE)
    wv = w[:, :, B_NOPE:].reshape(r, B_HEADS * B_V)
    return wk.astype(BF16), wv.astype(BF16)


def kernel(x, norm_pre, norm_post, w_in, a_q_norm, a_k_norm, b_q_norm, b_kv_norm, b_w_uq, b_w_ukv, c_rpb, w_out):
    batch, seq, d = x.shape
    depth = w_in.shape[0]
    assert seq % C_TQ == 0 and seq // GRID_W >= C_WROWS and w_in.shape[2] == W_IN_COLS - 64
    tabs = _rope_tables(seq)
    bias, pat_ids = _nbr_bias(c_rpb, seq)
    w_in_all = _prep_w_in(w_in)
    h = x.reshape(batch * seq, d)
    for l in range(depth):
        wuk, wuv = _prep_w_ukv(b_w_ukv[l])
        qa, ka, va, qb, kb, vb, qc, kc, vc, gate = _inproj(
            h, seq, l, norm_pre[l][None], w_in_all, _prep_w_uq(b_w_uq[l]), wuk, wuv,
            a_q_norm[l][None], a_k_norm[l][None], b_q_norm[l][None], b_kv_norm[l][None],
            tabs, tm=256)
        mix_a = _dense_attn(qa, ka, va, gate, seq, heads=A_HEADS, kv_heads=A_KV_HEADS,
                            dq=HEAD_DIM, dv=HEAD_DIM, gate_off=0, tq=256, name="attn_a")
        mix_b = _dense_attn(qb, kb, vb, gate, seq, heads=B_HEADS, kv_heads=B_HEADS,
                            dq=B_QK_PAD, dv=B_V, gate_off=A_Q, tq=256, name="attn_b")
        mix_c = _nbr_attn(qc, kc, vc, bias, pat_ids, l, gate, seq, A_Q + B_G)
        h = _outproj(mix_a, mix_b, mix_c, h, w_out[l].astype(BF16), norm_post[l][None], tm=512)
    return h.reshape(batch, seq, d)
```

```python
import functools

import jax
import jax.numpy as jnp
import numpy as np
from jax import lax
from jax.experimental import pallas as pl
from jax.experimental.pallas import tpu as pltpu

F32 = jnp.float32
BF16 = jnp.bfloat16

GRID_W = 64
HEAD_DIM = 128
A_HEADS = 8
A_KV_HEADS = 2
A_GROUP = A_HEADS // A_KV_HEADS
B_HEADS = 4
B_Q_LORA = 512
B_KV_LORA = 256
B_NOPE = 128
B_ROPE = 64
B_V = 128
B_QK_PAD = 256
C_HEADS = 4
WIN_ROWS = 8
WIN_COLS = 16
ROPE_THETA = 10000.0
NORM_EPS = 1e-6
NEG_BIG = -1e30
LOG2E = 1.4426950408889634

A_Q = A_HEADS * HEAD_DIM
A_KV = A_KV_HEADS * HEAD_DIM
B_G = B_HEADS * B_V
C_W = C_HEADS * HEAD_DIM
D_MIX = A_Q + B_G + C_W

OFF_AQ = 0
OFF_AK = OFF_AQ + A_Q
OFF_AV = OFF_AK + A_KV
OFF_AG = OFF_AV + A_KV
OFF_BCQ = OFF_AG + A_Q
OFF_BCKV = OFF_BCQ + B_Q_LORA
OFF_BG = OFF_BCKV + B_KV_LORA
OFF_CQ = OFF_BG + B_G
OFF_CK = OFF_CQ + C_W
OFF_CV = OFF_CK + C_W
OFF_CG = OFF_CV + C_W
OFF_BKR = OFF_CG + C_W
W_IN_COLS = OFF_BKR + 128

C_QROWS = 4
C_WROWS = 12
C_TQ = C_QROWS * GRID_W
C_TK = C_WROWS * GRID_W

VMEM_LIMIT_BYTES = 56 * 1024 * 1024


def _rms(x, gain):
    return x * lax.rsqrt(jnp.mean(x * x, axis=-1, keepdims=True) + NORM_EPS) * gain


def _rope(x, cos, sin_lo, sin_hi, half):
    up = pltpu.roll(x, 128 - half, 1)
    dn = pltpu.roll(x, half, 1)
    return x * cos + up * sin_lo + dn * sin_hi


def _silu(g):
    return g * (1.0 / (1.0 + jnp.exp(-g)))


_NT = (((1,), (1,)), ((), ()))


def _inproj_kernel(x_ref, gpre_ref, w_ref, wuq_ref, wuk_ref, wuv_ref,
                   aqn_ref, akn_ref, bqn_ref, bkvn_ref,
                   cos_a, slo_a, shi_a, cos_b, slo_b, shi_b,
                   qa_ref, ka_ref, va_ref, ga_ref, qb_ref, kb_ref, vb_ref, gb_ref,
                   qc_ref, kc_ref, vc_ref, gc_ref, h_scr):
    x = x_ref[...]
    h_scr[...] = (x * gpre_ref[...]).astype(BF16)
    inv_rms = lax.rsqrt(jnp.mean(x * x, axis=-1, keepdims=True) + NORM_EPS)
    tm = x.shape[0]

    def proj(off, width):
        return jnp.dot(h_scr[...], w_ref[:, off:off + width], preferred_element_type=F32) * inv_rms

    ones = jnp.ones((tm, HEAD_DIM), BF16)

    ca, sla, sha = cos_a[...], slo_a[...], shi_a[...]
    cb, slb, shb = cos_b[...], slo_b[...], shi_b[...]
    scale_a = HEAD_DIM ** -0.5 * LOG2E
    scale_b = (B_NOPE + B_ROPE) ** -0.5 * LOG2E

    cq = _rms(proj(OFF_BCQ, B_Q_LORA), bqn_ref[...]).astype(BF16)
    ckv = _rms(proj(OFF_BCKV, B_KV_LORA), bkvn_ref[...]).astype(BF16)
    kpe = _rope(proj(OFF_BKR, 128), cb, slb, shb, 16).astype(BF16)

    pq = proj(OFF_AQ, A_Q)
    for hd in range(A_HEADS):
        sl = slice(hd * HEAD_DIM, (hd + 1) * HEAD_DIM)
        q = _rope(_rms(pq[:, sl], aqn_ref[...]), ca, sla, sha, 32)
        qa_ref[hd] = (q * scale_a).astype(BF16)
    pk = proj(OFF_AK, A_KV)
    pv = proj(OFF_AV, A_KV)
    for hd in range(A_KV_HEADS):
        sl = slice(hd * HEAD_DIM, (hd + 1) * HEAD_DIM)
        ka_ref[hd] = _rope(_rms(pk[:, sl], akn_ref[...]), ca, sla, sha, 32).astype(BF16)
        va_ref[hd, :, 0:HEAD_DIM] = pv[:, sl].astype(BF16)
        va_ref[hd, :, HEAD_DIM:2 * HEAD_DIM] = ones
    ga = _silu(proj(OFF_AG, A_Q)).astype(BF16)
    for hd in range(A_HEADS):
        ga_ref[hd] = ga[:, hd * HEAD_DIM:(hd + 1) * HEAD_DIM]

    qb = jnp.dot(cq, wuq_ref[...], preferred_element_type=F32)
    kn = jnp.dot(ckv, wuk_ref[...], preferred_element_type=F32)
    vb = jnp.dot(ckv, wuv_ref[...], preferred_element_type=F32)
    gb = _silu(proj(OFF_BG, B_G)).astype(BF16)
    for hd in range(B_HEADS):
        lo = hd * B_QK_PAD
        qb_ref[hd, :, 0:B_NOPE] = (qb[:, lo:lo + B_NOPE] * scale_b).astype(BF16)
        qpe = _rope(qb[:, lo + B_NOPE:lo + B_QK_PAD], cb, slb, shb, 16)
        qb_ref[hd, :, B_NOPE:B_QK_PAD] = (qpe * scale_b).astype(BF16)
        kb_ref[hd, :, 0:B_NOPE] = kn[:, hd * B_NOPE:(hd + 1) * B_NOPE].astype(BF16)
        kb_ref[hd, :, B_NOPE:B_QK_PAD] = kpe
        vb_ref[hd, :, 0:B_V] = vb[:, hd * B_V:(hd + 1) * B_V].astype(BF16)
        vb_ref[hd, :, B_V:2 * B_V] = ones
        gb_ref[hd] = gb[:, hd * B_V:(hd + 1) * B_V]

    gc_ref[...] = _silu(proj(OFF_CG, C_W)).astype(BF16)
    qc_ref[...] = (proj(OFF_CQ, C_W) * scale_a).astype(BF16)
    pvc = proj(OFF_CV, C_W)
    for hd in range(C_HEADS):
        vc_ref[:, 2 * hd * HEAD_DIM:(2 * hd + 1) * HEAD_DIM] = (
            pvc[:, hd * HEAD_DIM:(hd + 1) * HEAD_DIM].astype(BF16))
        vc_ref[:, (2 * hd + 1) * HEAD_DIM:(2 * hd + 2) * HEAD_DIM] = ones
    kc_ref[...] = proj(OFF_CK, C_W).astype(BF16)


def _inproj(x2d, seq, layer, gpre, w_all, wuq, wuk, wuv, aqn, akn, bqn, bkvn, tabs, tm):
    t, d = x2d.shape
    n_s = seq // tm
    row = lambda i: (i, 0)
    fix = lambda i: (0, 0)
    tab = lambda i: (i % n_s, 0)

    def resident(a):
        return pl.BlockSpec(a.shape, fix, pipeline_mode=pl.Buffered(1))

    w_spec = pl.BlockSpec((None,) + w_all.shape[1:], lambda i: (layer, 0, 0),
                          pipeline_mode=pl.Buffered(1))

    head_major = ((A_HEADS, HEAD_DIM), (A_KV_HEADS, HEAD_DIM), (A_KV_HEADS, 2 * HEAD_DIM),
                  (A_HEADS, HEAD_DIM), (B_HEADS, B_QK_PAD), (B_HEADS, B_QK_PAD),
                  (B_HEADS, 2 * B_V), (B_HEADS, B_V))
    token_major = (C_W, C_W, 2 * C_W, C_W)
    out_specs = ([pl.BlockSpec((nh, tm, n), lambda i: (0, i, 0)) for nh, n in head_major]
                 + [pl.BlockSpec((tm, n), row) for n in token_major])
    out_shape = ([jax.ShapeDtypeStruct((nh, t, n), BF16) for nh, n in head_major]
                 + [jax.ShapeDtypeStruct((t, n), BF16) for n in token_major])
    return pl.pallas_call(
        _inproj_kernel,
        grid=(t // tm,),
        in_specs=[pl.BlockSpec((tm, d), row), resident(gpre), w_spec, resident(wuq),
                  resident(wuk), resident(wuv), resident(aqn), resident(akn), resident(bqn),
                  resident(bkvn)] + [pl.BlockSpec((tm, 128), tab)] * 6,
        out_specs=out_specs,
        out_shape=out_shape,
        scratch_shapes=[pltpu.VMEM((tm, d), BF16)],
        compiler_params=pltpu.CompilerParams(dimension_semantics=("parallel",),
                                             vmem_limit_bytes=VMEM_LIMIT_BYTES),
        name="inproj",
    )(x2d, gpre, w_all, wuq, wuk, wuv, aqn, akn, bqn, bkvn, *tabs)


def _softmax_pv(s, v_ext, dv):
    m = jnp.max(s, axis=-1, keepdims=True)
    p = jnp.exp2(s - m).astype(BF16)
    oe = jnp.dot(p, v_ext, preferred_element_type=F32)
    return oe[:, :dv] * (1.0 / oe[:, dv:])


def _dense_attn_kernel(q_ref, k_ref, v_ref, g_ref, o_ref, s_a, s_b, *, shared_kv, dv, tq):
    heads, seq, _ = q_ref.shape
    n_t = seq // tq
    bufs = (s_a, s_b)

    def qk(hd, u, s_ref):
        k = k_ref[0] if shared_kv else k_ref[hd]
        s_ref[...] = lax.dot_general(q_ref[hd, u * tq:(u + 1) * tq, :], k, _NT,
                                     preferred_element_type=F32)

    def finish(hd, u, s_ref):
        v = v_ref[0] if shared_kv else v_ref[hd]
        o = _softmax_pv(s_ref[...], v, dv)
        rows = slice(u * tq, (u + 1) * tq)
        o_ref[hd, rows, :] = (o * g_ref[hd, rows, :].astype(F32)).astype(BF16)

    def body(hd, carry):
        nxt = jnp.minimum(hd + 1, heads - 1)
        for u in range(n_t):
            if u + 1 < n_t:
                qk(hd, u + 1, bufs[(u + 1) % 2])
            else:
                qk(nxt, 0, bufs[0])
            finish(hd, u, bufs[u % 2])
        return carry

    qk(0, 0, s_a)
    lax.fori_loop(0, heads, body, 0)


def _dense_attn(q, k, v_ext, gate, seq, *, dv, tq, name):
    heads, t, dq = q.shape
    kv_heads = k.shape[0]
    shared_kv = kv_heads < heads
    assert (seq // tq) % 2 == 0
    per_step = heads // kv_heads if shared_kv else heads
    kv_per_step = 1 if shared_kv else per_step
    blk = lambda b, h: (h, b, 0)
    kern = functools.partial(_dense_attn_kernel, shared_kv=shared_kv, dv=dv, tq=tq)
    return pl.pallas_call(
        kern,
        grid=(t // seq, heads // per_step),
        in_specs=[pl.BlockSpec((per_step, seq, dq), blk),
                  pl.BlockSpec((kv_per_step, seq, dq), blk),
                  pl.BlockSpec((kv_per_step, seq, 2 * dv), blk),
                  pl.BlockSpec((per_step, seq, dv), blk)],
        out_specs=pl.BlockSpec((per_step, seq, dv), blk),
        out_shape=jax.ShapeDtypeStruct((heads, t, dv), BF16),
        scratch_shapes=[pltpu.VMEM((tq, seq), F32), pltpu.VMEM((tq, seq), F32)],
        compiler_params=pltpu.CompilerParams(dimension_semantics=("parallel", "parallel"),
                                             vmem_limit_bytes=VMEM_LIMIT_BYTES),
        name=name,
    )(q, k, v_ext, gate)


def _nbr_attn_kernel(pid_ref, q0_ref, qn_ref, kn_ref, v_ref, bias_ref, g_ref, o_ref, s_a, s_b,
                     *, max_ws, nb, n_blocks):
    del pid_ref
    g = pl.program_id(0)

    def window_start(i):
        ws = jnp.clip(C_QROWS * i - WIN_ROWS // 2, 0, max_ws)
        return pl.multiple_of(ws * GRID_W, GRID_W)

    def qk(q_ref, i, s_ref, hd):
        start = window_start(i)
        sl = slice(hd * HEAD_DIM, (hd + 1) * HEAD_DIM)
        s_ref[hd] = lax.dot_general(q_ref[:, sl], kn_ref[pl.ds(start, C_TK), sl], _NT,
                                    preferred_element_type=F32)

    def finish(s_ref, hd):
        start = window_start(g % nb)
        sl = slice(hd * HEAD_DIM, (hd + 1) * HEAD_DIM)
        vw = v_ref[pl.ds(start, C_TK), 2 * hd * HEAD_DIM:2 * (hd + 1) * HEAD_DIM]
        o = _softmax_pv(s_ref[hd] + bias_ref[hd], vw, HEAD_DIM) * g_ref[:, sl].astype(F32)
        o_ref[:, sl] = o.astype(BF16)

    i_next = jnp.minimum(g + 1, n_blocks - 1) % nb

    @pl.when(g == 0)
    def _():
        for hd in range(C_HEADS):
            qk(q0_ref, 0, s_a, hd)

    def step(s_cur, s_next):
        for hd in range(C_HEADS):
            qk(qn_ref, i_next, s_next, hd)
            finish(s_cur, hd)

    @pl.when(g % 2 == 0)
    def _():
        step(s_a, s_b)

    @pl.when(g % 2 == 1)
    def _():
        step(s_b, s_a)


def _nbr_attn(q, k, v, bias, pat_ids, layer, gate, seq):
    t = q.shape[0]
    nb = seq // C_TQ
    n_blocks = t // C_TQ
    assert nb >= 2
    kern = functools.partial(_nbr_attn_kernel, max_ws=seq // GRID_W - C_WROWS, nb=nb,
                             n_blocks=n_blocks)
    nxt = lambda g: jnp.minimum(g + 1, n_blocks - 1)
    grid_spec = pltpu.PrefetchScalarGridSpec(
        num_scalar_prefetch=1,
        grid=(n_blocks,),
        in_specs=[pl.BlockSpec((C_TQ, C_W), lambda g, pid: (0, 0)),
                  pl.BlockSpec((C_TQ, C_W), lambda g, pid: (nxt(g), 0)),
                  pl.BlockSpec((seq, C_W), lambda g, pid: (nxt(g) // nb, 0)),
                  pl.BlockSpec((seq, 2 * C_W), lambda g, pid: (g // nb, 0)),
                  pl.BlockSpec((None, None, C_HEADS, C_TQ, C_TK),
                               lambda g, pid: (pid[g % nb], layer, 0, 0, 0)),
                  pl.BlockSpec((C_TQ, C_W), lambda g, pid: (g, 0))],
        out_specs=pl.BlockSpec((C_TQ, C_W), lambda g, pid: (g, 0)),
        scratch_shapes=[pltpu.VMEM((C_HEADS, C_TQ, C_TK), F32)] * 2)
    return pl.pallas_call(
        kern,
        grid_spec=grid_spec,
        out_shape=jax.ShapeDtypeStruct((t, C_W), BF16),
        compiler_params=pltpu.CompilerParams(dimension_semantics=("arbitrary",),
                                             vmem_limit_bytes=VMEM_LIMIT_BYTES),
        name="attn_c",
    )(pat_ids, q, q, k, v, bias, gate)


def _nbr_bias(rpb, seq):
    depth, heads, n_dr, n_dc = rpb.shape
    g = depth * heads
    rows = seq // GRID_W
    nb = seq // C_TQ
    neg = NEG_BIG * LOG2E
    lead = GRID_W - WIN_COLS
    u = jnp.pad(rpb.reshape(g, n_dr, n_dc) * LOG2E, ((0, 0), (0, 0), (lead, 2 * GRID_W - n_dc - lead)))
    skew = jnp.tile(u, (1, 1, GRID_W))[:, :, GRID_W - 1:GRID_W - 1 + GRID_W * (2 * GRID_W - 1)]
    vals = skew.reshape(g, n_dr, GRID_W, 2 * GRID_W - 1)[..., :GRID_W]
    qc = np.arange(GRID_W)[:, None]
    kc = np.arange(GRID_W)[None, :]
    c0 = np.clip(qc - WIN_COLS // 2, 0, GRID_W - WIN_COLS)
    valid_c = (kc >= c0) & (kc < c0 + WIN_COLS)
    slabs = jnp.where(valid_c[None, None], vals, neg)
    blk = np.arange(nb)[:, None, None]
    r = C_QROWS * blk + np.arange(C_QROWS)[None, :, None]
    ws = np.clip(C_QROWS * blk - WIN_ROWS // 2, 0, rows - C_WROWS)
    kr = ws + np.arange(C_WROWS)[None, None, :]
    r0 = np.clip(r - WIN_ROWS // 2, 0, rows - WIN_ROWS)
    valid_r = (kr >= r0) & (kr < r0 + WIN_ROWS)
    slab_id = np.where(valid_r, kr - r + (WIN_ROWS - 1), -1)
    pats, pat_ids = np.unique(slab_id.reshape(nb, -1), axis=0, return_inverse=True)
    pats = np.where(pats < 0, n_dr, pats).reshape(-1, C_QROWS, C_WROWS)
    n_pat = pats.shape[0]
    slabs = jnp.concatenate([slabs, jnp.full((g, 1, GRID_W, GRID_W), neg, F32)], axis=1)
    pair_src = jnp.concatenate([slabs, slabs], axis=-1)
    table = tuple(tuple(tuple(int(s) for s in row) for row in pat) for pat in pats)
    bias = pl.pallas_call(
        functools.partial(_nbr_bias_kernel, table=table),
        grid=(g,),
        in_specs=[pl.BlockSpec((None, n_dr + 1, GRID_W, 2 * GRID_W), lambda i: (i, 0, 0, 0))],
        out_specs=pl.BlockSpec((n_pat, None, C_TQ, C_TK), lambda i: (0, i, 0, 0)),
        out_shape=jax.ShapeDtypeStruct((n_pat, g, C_TQ, C_TK), F32),
        compiler_params=pltpu.CompilerParams(dimension_semantics=("parallel",)),
        name="nbr_bias",
    )(pair_src)
    return (bias.reshape(n_pat, depth, heads, C_TQ, C_TK),
            jnp.asarray(pat_ids.reshape(nb), jnp.int32))


def _nbr_bias_kernel(src_ref, o_ref, *, table):
    left = lax.broadcasted_iota(jnp.int32, (GRID_W, 2 * GRID_W), 1) < GRID_W
    for p, pat in enumerate(table):
        for q, row in enumerate(pat):
            for kp in range(len(row) // 2):
                tile = jnp.where(left, src_ref[row[2 * kp]], src_ref[row[2 * kp + 1]])
                o_ref[p, q * GRID_W:(q + 1) * GRID_W, 2 * kp * GRID_W:2 * (kp + 1) * GRID_W] = tile


def _outproj_kernel(ma_ref, mb_ref, mc_ref, x_ref, w_ref, gpost_ref, o_ref):
    ma = jnp.concatenate([ma_ref[hd] for hd in range(A_HEADS)], axis=1)
    mb = jnp.concatenate([mb_ref[hd] for hd in range(B_HEADS)], axis=1)
    y = jnp.dot(ma, w_ref[0:A_Q, :], preferred_element_type=F32)
    y += jnp.dot(mb, w_ref[A_Q:A_Q + B_G, :], preferred_element_type=F32)
    y += jnp.dot(mc_ref[...], w_ref[A_Q + B_G:D_MIX, :], preferred_element_type=F32)
    o_ref[...] = x_ref[...] + _rms(y, gpost_ref[...])


def _outproj(ma, mb, mc, x2d, w, gpost, tm):
    t, d = x2d.shape
    row = lambda i: (i, 0)
    fix = lambda i: (0, 0)
    return pl.pallas_call(
        _outproj_kernel,
        grid=(t // tm,),
        in_specs=[pl.BlockSpec((A_HEADS, tm, HEAD_DIM), lambda i: (0, i, 0)),
                  pl.BlockSpec((B_HEADS, tm, B_V), lambda i: (0, i, 0)),
                  pl.BlockSpec((tm, C_W), row), pl.BlockSpec((tm, d), row),
                  pl.BlockSpec(w.shape, fix, pipeline_mode=pl.Buffered(1)),
                  pl.BlockSpec(gpost.shape, fix, pipeline_mode=pl.Buffered(1))],
        out_specs=pl.BlockSpec((tm, d), row),
        out_shape=jax.ShapeDtypeStruct((t, d), F32),
        compiler_params=pltpu.CompilerParams(dimension_semantics=("parallel",),
                                             vmem_limit_bytes=VMEM_LIMIT_BYTES),
        name="outproj",
    )(ma, mb, mc, x2d, w, gpost)


def _rope_tables(seq):
    t = np.arange(seq)
    pos = np.stack([t // GRID_W, t % GRID_W], axis=0).astype(np.float64)

    def tables(n, lanes_used):
        j = np.arange(128)
        axis = np.minimum(j // n, 1)
        i = j % (n // 2)
        inv_freq = 1.0 / (ROPE_THETA ** (np.arange(0, n, 2, dtype=np.float64) / n))
        ang = pos[axis, :].T * inv_freq[i][None, :]
        used = (j < lanes_used)[None, :]
        low = ((j % n) < n // 2)[None, :]
        cos = np.where(used, np.cos(ang), 0.0)
        sin = np.where(used, np.sin(ang), 0.0)
        return tuple(jnp.asarray(a, F32) for a in (cos, np.where(low, -sin, 0.0), np.where(low, 0.0, sin)))

    return tables(HEAD_DIM // 2, 128) + tables(B_ROPE // 2, B_ROPE)


def _w_in_repack_kernel(w_ref, o_ref):
    kr_lo = OFF_BG
    kr_hi = kr_lo + B_ROPE
    tc = w_ref.shape[1]
    chunk = 256
    for lo in range(0, kr_lo, chunk):
        o_ref[:, lo:lo + chunk] = w_ref[lo:lo + chunk, :].T.astype(BF16)
    for lo in range(kr_lo, OFF_BKR, chunk):
        o_ref[:, lo:lo + chunk] = w_ref[lo + B_ROPE:lo + B_ROPE + chunk, :].T.astype(BF16)
    tail = jnp.concatenate([w_ref[kr_lo:kr_hi, :], jnp.zeros((128 - B_ROPE, tc), F32)], axis=0)
    o_ref[:, OFF_BKR:W_IN_COLS] = tail.T.astype(BF16)


def _prep_w_in(w, tc=256):
    depth, d, n = w.shape
    wt = jnp.swapaxes(w, 1, 2)
    return pl.pallas_call(
        _w_in_repack_kernel,
        grid=(depth, d // tc),
        in_specs=[pl.BlockSpec((None, n, tc), lambda l, i: (l, 0, i))],
        out_specs=pl.BlockSpec((None, tc, W_IN_COLS), lambda l, i: (l, i, 0)),
        out_shape=jax.ShapeDtypeStruct((depth, d, W_IN_COLS), BF16),
        compiler_params=pltpu.CompilerParams(dimension_semantics=("parallel", "parallel"),
                                             vmem_limit_bytes=VMEM_LIMIT_BYTES),
        name="w_in_repack",
    )(wt)


def _prep_w_uq(w):
    r = w.shape[0]
    w = w.reshape(r, B_HEADS, B_NOPE + B_ROPE)
    w = jnp.pad(w, ((0, 0), (0, 0), (0, B_QK_PAD - B_NOPE - B_ROPE)))
    return w.reshape(r, B_HEADS * B_QK_PAD).astype(BF16)


def _prep_w_ukv(w):
    r = w.shape[0]
    w = w.reshape(r, B_HEADS, B_NOPE + B_V)
    wk = w[:, :, :B_NOPE].reshape(r, B_HEADS * B_NOPE)
    wv = w[:, :, B_NOPE:].reshape(r, B_HEADS * B_V)
    return wk.astype(BF16), wv.astype(BF16)


def kernel(x, norm_pre, norm_post, w_in, a_q_norm, a_k_norm, b_q_norm, b_kv_norm, b_w_uq, b_w_ukv, c_rpb, w_out):
    batch, seq, d = x.shape
    depth = w_in.shape[0]
    assert seq % C_TQ == 0 and seq // GRID_W >= C_WROWS and w_in.shape[2] == W_IN_COLS - 64
    tabs = _rope_tables(seq)
    bias, pat_ids = _nbr_bias(c_rpb, seq)
    w_in_all = _prep_w_in(w_in)
    h = x.reshape(batch * seq, d)
    for l in range(depth):
        wuk, wuv = _prep_w_ukv(b_w_ukv[l])
        qa, ka, va, ga, qb, kb, vb, gb, qc, kc, vc, gc = _inproj(
            h, seq, l, norm_pre[l][None], w_in_all, _prep_w_uq(b_w_uq[l]), wuk, wuv,
            a_q_norm[l][None], a_k_norm[l][None], b_q_norm[l][None], b_kv_norm[l][None],
            tabs, tm=256)
        mix_a = _dense_attn(qa, ka, va, ga, seq, dv=HEAD_DIM, tq=256, name="attn_a")
        mix_b = _dense_attn(qb, kb, vb, gb, seq, dv=B_V, tq=256, name="attn_b")
        mix_c = _nbr_attn(qc, kc, vc, bias, pat_ids, l, gc, seq)
        h = _outproj(mix_a, mix_b, mix_c, h, w_out[l].astype(BF16), norm_post[l][None], tm=512)
    return h.reshape(batch, seq, d)
```

```python
import functools

import jax
import jax.numpy as jnp
import numpy as np
from jax import lax
from jax.experimental import pallas as pl
from jax.experimental.pallas import tpu as pltpu

F32 = jnp.float32
BF16 = jnp.bfloat16

GRID_W = 64
HEAD_DIM = 128
A_HEADS = 8
A_KV_HEADS = 2
A_GROUP = A_HEADS // A_KV_HEADS
B_HEADS = 4
B_Q_LORA = 512
B_KV_LORA = 256
B_NOPE = 128
B_ROPE = 64
B_V = 128
B_QK_PAD = 256
C_HEADS = 4
WIN_ROWS = 8
WIN_COLS = 16
ROPE_THETA = 10000.0
NORM_EPS = 1e-6
NEG_BIG = -1e30
LOG2E = 1.4426950408889634

A_Q = A_HEADS * HEAD_DIM
A_KV = A_KV_HEADS * HEAD_DIM
B_G = B_HEADS * B_V
C_W = C_HEADS * HEAD_DIM
D_MIX = A_Q + B_G + C_W

OFF_AQ = 0
OFF_AK = OFF_AQ + A_Q
OFF_AV = OFF_AK + A_KV
OFF_AG = OFF_AV + A_KV
OFF_BCQ = OFF_AG + A_Q
OFF_BCKV = OFF_BCQ + B_Q_LORA
OFF_BG = OFF_BCKV + B_KV_LORA
OFF_CQ = OFF_BG + B_G
OFF_CK = OFF_CQ + C_W
OFF_CV = OFF_CK + C_W
OFF_CG = OFF_CV + C_W
OFF_BKR = OFF_CG + C_W
W_IN_COLS = OFF_BKR + 128

C_QROWS = 4
C_WROWS = 12
C_TQ = C_QROWS * GRID_W
C_TK = C_WROWS * GRID_W

VMEM_LIMIT_BYTES = 56 * 1024 * 1024


def _rms(x, gain):
    return x * lax.rsqrt(jnp.mean(x * x, axis=-1, keepdims=True) + NORM_EPS) * gain


def _rope(x, cos, sin_lo, sin_hi, half):
    up = pltpu.roll(x, 128 - half, 1)
    dn = pltpu.roll(x, half, 1)
    return x * cos + up * sin_lo + dn * sin_hi


def _silu(g):
    return g * (1.0 / (1.0 + jnp.exp(-g)))


_NT = (((1,), (1,)), ((), ()))


def _inproj_kernel(x_ref, gpre_ref, w_ref, wuq_ref, wuk_ref, wuv_ref,
                   aqn_ref, akn_ref, bqn_ref, bkvn_ref,
                   cos_a, slo_a, shi_a, cos_b, slo_b, shi_b,
                   qa_ref, ka_ref, va_ref, ga_ref, qb_ref, kb_ref, vb_ref, gb_ref,
                   qc_ref, kc_ref, vc_ref, gc_ref, h_scr):
    x = x_ref[...]
    h_scr[...] = (x * gpre_ref[...]).astype(BF16)
    inv_rms = lax.rsqrt(jnp.mean(x * x, axis=-1, keepdims=True) + NORM_EPS)
    tm = x.shape[0]

    def proj(off, width):
        return jnp.dot(h_scr[...], w_ref[:, off:off + width], preferred_element_type=F32) * inv_rms

    ones = jnp.ones((tm, HEAD_DIM), BF16)

    ca, sla, sha = cos_a[...], slo_a[...], shi_a[...]
    cb, slb, shb = cos_b[...], slo_b[...], shi_b[...]
    scale_a = HEAD_DIM ** -0.5 * LOG2E
    scale_b = (B_NOPE + B_ROPE) ** -0.5 * LOG2E

    cq = _rms(proj(OFF_BCQ, B_Q_LORA), bqn_ref[...]).astype(BF16)
    ckv = _rms(proj(OFF_BCKV, B_KV_LORA), bkvn_ref[...]).astype(BF16)
    kpe = _rope(proj(OFF_BKR, 128), cb, slb, shb, 16).astype(BF16)

    pq = proj(OFF_AQ, A_Q)
    for hd in range(A_HEADS):
        sl = slice(hd * HEAD_DIM, (hd + 1) * HEAD_DIM)
        q = _rope(_rms(pq[:, sl], aqn_ref[...]), ca, sla, sha, 32)
        qa_ref[hd] = (q * scale_a).astype(BF16)
    pk = proj(OFF_AK, A_KV)
    pv = proj(OFF_AV, A_KV)
    for hd in range(A_KV_HEADS):
        sl = slice(hd * HEAD_DIM, (hd + 1) * HEAD_DIM)
        ka_ref[hd] = _rope(_rms(pk[:, sl], akn_ref[...]), ca, sla, sha, 32).astype(BF16)
        va_ref[hd, :, 0:HEAD_DIM] = pv[:, sl].astype(BF16)
        va_ref[hd, :, HEAD_DIM:2 * HEAD_DIM] = ones
    ga = _silu(proj(OFF_AG, A_Q)).astype(BF16)
    for hd in range(A_HEADS):
        ga_ref[hd] = ga[:, hd * HEAD_DIM:(hd + 1) * HEAD_DIM]

    qb = jnp.dot(cq, wuq_ref[...], preferred_element_type=F32)
    kn = jnp.dot(ckv, wuk_ref[...], preferred_element_type=F32)
    vb = jnp.dot(ckv, wuv_ref[...], preferred_element_type=F32)
    gb = _silu(proj(OFF_BG, B_G)).astype(BF16)
    for hd in range(B_HEADS):
        lo = hd * B_QK_PAD
        qb_ref[hd, :, 0:B_NOPE] = (qb[:, lo:lo + B_NOPE] * scale_b).astype(BF16)
        qpe = _rope(qb[:, lo + B_NOPE:lo + B_QK_PAD], cb, slb, shb, 16)
        qb_ref[hd, :, B_NOPE:B_QK_PAD] = (qpe * scale_b).astype(BF16)
        kb_ref[hd, :, 0:B_NOPE] = kn[:, hd * B_NOPE:(hd + 1) * B_NOPE].astype(BF16)
        kb_ref[hd, :, B_NOPE:B_QK_PAD] = kpe
        vb_ref[hd, :, 0:B_V] = vb[:, hd * B_V:(hd + 1) * B_V].astype(BF16)
        vb_ref[hd, :, B_V:2 * B_V] = ones
        gb_ref[hd] = gb[:, hd * B_V:(hd + 1) * B_V]

    gc_ref[...] = _silu(proj(OFF_CG, C_W)).astype(BF16)
    qc_ref[...] = (proj(OFF_CQ, C_W) * scale_a).astype(BF16)
    pvc = proj(OFF_CV, C_W)
    for hd in range(C_HEADS):
        vc_ref[:, 2 * hd * HEAD_DIM:(2 * hd + 1) * HEAD_DIM] = (
            pvc[:, hd * HEAD_DIM:(hd + 1) * HEAD_DIM].astype(BF16))
        vc_ref[:, (2 * hd + 1) * HEAD_DIM:(2 * hd + 2) * HEAD_DIM] = ones
    kc_ref[...] = proj(OFF_CK, C_W).astype(BF16)


def _inproj(x2d, seq, layer, gpre, w_all, wuq, wuk, wuv, aqn, akn, bqn, bkvn, tabs, tm):
    t, d = x2d.shape
    n_s = seq // tm
    row = lambda i: (i, 0)
    fix = lambda i: (0, 0)
    tab = lambda i: (i % n_s, 0)

    def resident(a):
        return pl.BlockSpec(a.shape, fix, pipeline_mode=pl.Buffered(1))

    w_spec = pl.BlockSpec((None,) + w_all.shape[1:], lambda i: (layer, 0, 0),
                          pipeline_mode=pl.Buffered(1))

    head_major = ((A_HEADS, HEAD_DIM), (A_KV_HEADS, HEAD_DIM), (A_KV_HEADS, 2 * HEAD_DIM),
                  (A_HEADS, HEAD_DIM), (B_HEADS, B_QK_PAD), (B_HEADS, B_QK_PAD),
                  (B_HEADS, 2 * B_V), (B_HEADS, B_V))
    token_major = (C_W, C_W, 2 * C_W, C_W)
    out_specs = ([pl.BlockSpec((nh, tm, n), lambda i: (0, i, 0)) for nh, n in head_major]
                 + [pl.BlockSpec((tm, n), row) for n in token_major])
    out_shape = ([jax.ShapeDtypeStruct((nh, t, n), BF16) for nh, n in head_major]
                 + [jax.ShapeDtypeStruct((t, n), BF16) for n in token_major])
    return pl.pallas_call(
        _inproj_kernel,
        grid=(t // tm,),
        in_specs=[pl.BlockSpec((tm, d), row), resident(gpre), w_spec, resident(wuq),
                  resident(wuk), resident(wuv), resident(aqn), resident(akn), resident(bqn),
                  resident(bkvn)] + [pl.BlockSpec((tm, 128), tab)] * 6,
        out_specs=out_specs,
        out_shape=out_shape,
        scratch_shapes=[pltpu.VMEM((tm, d), BF16)],
        compiler_params=pltpu.CompilerParams(dimension_semantics=("parallel",),
                                             vmem_limit_bytes=VMEM_LIMIT_BYTES),
        name="inproj",
    )(x2d, gpre, w_all, wuq, wuk, wuv, aqn, akn, bqn, bkvn, *tabs)


def _softmax_pv(s, v_ext, dv):
    m = jnp.max(s, axis=-1, keepdims=True)
    p = jnp.exp2(s - m).astype(BF16)
    oe = jnp.dot(p, v_ext, preferred_element_type=F32)
    return oe[:, :dv] * (1.0 / oe[:, dv:])


def _dense_attn_kernel(q_ref, k_ref, v_ref, g_ref, o_ref, s_a, s_b, *, shared_kv, dv, tq):
    heads, seq, _ = q_ref.shape
    n_t = seq // tq
    bufs = (s_a, s_b)

    def qk(hd, u, s_ref):
        k = k_ref[0] if shared_kv else k_ref[hd]
        s_ref[...] = lax.dot_general(q_ref[hd, u * tq:(u + 1) * tq, :], k, _NT,
                                     preferred_element_type=F32)

    def finish(hd, u, s_ref):
        v = v_ref[0] if shared_kv else v_ref[hd]
        o = _softmax_pv(s_ref[...], v, dv)
        rows = slice(u * tq, (u + 1) * tq)
        o_ref[hd, rows, :] = (o * g_ref[hd, rows, :].astype(F32)).astype(BF16)

    def body(hd, carry):
        nxt = jnp.minimum(hd + 1, heads - 1)
        for u in range(n_t):
            if u + 1 < n_t:
                qk(hd, u + 1, bufs[(u + 1) % 2])
            else:
                qk(nxt, 0, bufs[0])
            finish(hd, u, bufs[u % 2])
        return carry

    qk(0, 0, s_a)
    lax.fori_loop(0, heads, body, 0)


def _dense_attn(q, k, v_ext, gate, seq, *, dv, tq, name):
    heads, t, dq = q.shape
    kv_heads = k.shape[0]
    shared_kv = kv_heads < heads
    assert (seq // tq) % 2 == 0
    per_step = heads // kv_heads if shared_kv else heads
    kv_per_step = 1 if shared_kv else per_step
    blk = lambda b, h: (h, b, 0)
    kern = functools.partial(_dense_attn_kernel, shared_kv=shared_kv, dv=dv, tq=tq)
    return pl.pallas_call(
        kern,
        grid=(t // seq, heads // per_step),
        in_specs=[pl.BlockSpec((per_step, seq, dq), blk),
                  pl.BlockSpec((kv_per_step, seq, dq), blk),
                  pl.BlockSpec((kv_per_step, seq, 2 * dv), blk),
                  pl.BlockSpec((per_step, seq, dv), blk)],
        out_specs=pl.BlockSpec((per_step, seq, dv), blk),
        out_shape=jax.ShapeDtypeStruct((heads, t, dv), BF16),
        scratch_shapes=[pltpu.VMEM((tq, seq), F32), pltpu.VMEM((tq, seq), F32)],
        compiler_params=pltpu.CompilerParams(dimension_semantics=("parallel", "parallel"),
                                             vmem_limit_bytes=VMEM_LIMIT_BYTES),
        name=name,
    )(q, k, v_ext, gate)


def _nbr_attn_kernel(pid_ref, q_ref, qn_ref, k_ref, kn_ref, v_ref, bias0_ref, bias1_ref, g_ref,
                     o_ref, s_a, s_b, *, max_ws, nb):
    del pid_ref
    j = pl.program_id(0)
    i0 = (2 * j) % nb

    def window_start(i):
        ws = jnp.clip(C_QROWS * i - WIN_ROWS // 2, 0, max_ws)
        return pl.multiple_of(ws * GRID_W, GRID_W)

    def qk(q, keys_ref, i, s_ref, hd):
        sl = slice(hd * HEAD_DIM, (hd + 1) * HEAD_DIM)
        s_ref[hd] = lax.dot_general(q[:, sl], keys_ref[pl.ds(window_start(i), C_TK), sl], _NT,
                                    preferred_element_type=F32)

    def finish(s_ref, bias_ref, i, rows, hd):
        sl = slice(hd * HEAD_DIM, (hd + 1) * HEAD_DIM)
        vw = v_ref[pl.ds(window_start(i), C_TK), 2 * hd * HEAD_DIM:2 * (hd + 1) * HEAD_DIM]
        o = _softmax_pv(s_ref[hd] + bias_ref[hd], vw, HEAD_DIM) * g_ref[rows, sl].astype(F32)
        o_ref[rows, sl] = o.astype(BF16)

    first, second = slice(0, C_TQ), slice(C_TQ, 2 * C_TQ)

    @pl.when(j == 0)
    def _():
        for hd in range(C_HEADS):
            qk(q_ref[first, :], k_ref, i0, s_a, hd)

    for hd in range(C_HEADS):
        qk(q_ref[second, :], k_ref, i0 + 1, s_b, hd)
        finish(s_a, bias0_ref, i0, first, hd)
    for hd in range(C_HEADS):
        qk(qn_ref[...], kn_ref, (i0 + 2) % nb, s_a, hd)
        finish(s_b, bias1_ref, i0 + 1, second, hd)


def _nbr_attn(q, k, v, bias, pat_ids, layer, gate, seq):
    t = q.shape[0]
    nb = seq // C_TQ
    n_blocks = t // C_TQ
    assert nb % 2 == 0
    kern = functools.partial(_nbr_attn_kernel, max_ws=seq // GRID_W - C_WROWS, nb=nb)
    nxt = lambda j: jnp.minimum(2 * j + 2, n_blocks - 1)
    bias_spec = lambda off: pl.BlockSpec(
        (None, None, C_HEADS, C_TQ, C_TK), lambda j, pid: (pid[(2 * j + off) % nb], layer, 0, 0, 0))
    grid_spec = pltpu.PrefetchScalarGridSpec(
        num_scalar_prefetch=1,
        grid=(n_blocks // 2,),
        in_specs=[pl.BlockSpec((2 * C_TQ, C_W), lambda j, pid: (j, 0)),
                  pl.BlockSpec((C_TQ, C_W), lambda j, pid: (nxt(j), 0)),
                  pl.BlockSpec((seq, C_W), lambda j, pid: (2 * j // nb, 0)),
                  pl.BlockSpec((seq, C_W), lambda j, pid: (nxt(j) // nb, 0)),
                  pl.BlockSpec((seq, 2 * C_W), lambda j, pid: (2 * j // nb, 0)),
                  bias_spec(0), bias_spec(1),
                  pl.BlockSpec((2 * C_TQ, C_W), lambda j, pid: (j, 0))],
        out_specs=pl.BlockSpec((2 * C_TQ, C_W), lambda j, pid: (j, 0)),
        scratch_shapes=[pltpu.VMEM((C_HEADS, C_TQ, C_TK), F32)] * 2)
    return pl.pallas_call(
        kern,
        grid_spec=grid_spec,
        out_shape=jax.ShapeDtypeStruct((t, C_W), BF16),
        compiler_params=pltpu.CompilerParams(dimension_semantics=("arbitrary",),
                                             vmem_limit_bytes=VMEM_LIMIT_BYTES),
        name="attn_c",
    )(pat_ids, q, q, k, k, v, bias, bias, gate)


def _nbr_bias(rpb, seq):
    depth, heads, n_dr, n_dc = rpb.shape
    g = depth * heads
    rows = seq // GRID_W
    nb = seq // C_TQ
    neg = NEG_BIG * LOG2E
    lead = GRID_W - WIN_COLS
    u = jnp.pad(rpb.reshape(g, n_dr, n_dc) * LOG2E,
                ((0, 0), (0, 1), (lead, 2 * GRID_W - n_dc - lead)))
    blk = np.arange(nb)[:, None, None]
    r = C_QROWS * blk + np.arange(C_QROWS)[None, :, None]
    ws = np.clip(C_QROWS * blk - WIN_ROWS // 2, 0, rows - C_WROWS)
    kr = ws + np.arange(C_WROWS)[None, None, :]
    r0 = np.clip(r - WIN_ROWS // 2, 0, rows - WIN_ROWS)
    valid_r = (kr >= r0) & (kr < r0 + WIN_ROWS)
    slab_id = np.where(valid_r, kr - r + (WIN_ROWS - 1), -1)
    pats, pat_ids = np.unique(slab_id.reshape(nb, -1), axis=0, return_inverse=True)
    pats = np.where(pats < 0, n_dr, pats).reshape(-1, C_QROWS, C_WROWS)
    n_pat = pats.shape[0]
    table = tuple(tuple(tuple(int(s) for s in row) for row in pat) for pat in pats)
    bias = pl.pallas_call(
        functools.partial(_nbr_bias_kernel, table=table, n_dr=n_dr, neg=neg),
        grid=(g,),
        in_specs=[pl.BlockSpec((None, n_dr + 1, 2 * GRID_W), lambda i: (i, 0, 0))],
        out_specs=pl.BlockSpec((n_pat, None, C_TQ, C_TK), lambda i: (0, i, 0, 0)),
        out_shape=jax.ShapeDtypeStruct((n_pat, g, C_TQ, C_TK), F32),
        scratch_shapes=[pltpu.VMEM((n_dr + 1, GRID_W, 2 * GRID_W), F32)],
        compiler_params=pltpu.CompilerParams(dimension_semantics=("parallel",)),
        name="nbr_bias",
    )(u)
    return (bias.reshape(n_pat, depth, heads, C_TQ, C_TK),
            jnp.asarray(pat_ids.reshape(nb), jnp.int32))


def _nbr_bias_kernel(u_ref, o_ref, slab_scr, *, table, n_dr, neg):
    shape = (GRID_W, 2 * GRID_W)
    lane = lax.broadcasted_iota(jnp.int32, shape, 1)
    qc = lax.broadcasted_iota(jnp.int32, shape, 0)
    kc = lane % GRID_W
    left = lane < GRID_W
    c0 = jnp.clip(qc - WIN_COLS // 2, 0, GRID_W - WIN_COLS)
    valid = (kc >= c0) & (kc < c0 + WIN_COLS)
    for dr in range(n_dr):
        rows = jnp.broadcast_to(u_ref[dr:dr + 1, :], shape)
        skew = pltpu.roll(rows, GRID_W + 1, 1, stride=1, stride_axis=0)
        both = jnp.where(left, skew, pltpu.roll(skew, GRID_W, 1))
        slab_scr[dr] = jnp.where(valid, both, neg)
    slab_scr[n_dr] = jnp.full(shape, neg, F32)
    for p, pat in enumerate(table):
        for q, row in enumerate(pat):
            for kp in range(len(row) // 2):
                tile = jnp.where(left, slab_scr[row[2 * kp]], slab_scr[row[2 * kp + 1]])
                o_ref[p, q * GRID_W:(q + 1) * GRID_W, 2 * kp * GRID_W:2 * (kp + 1) * GRID_W] = tile


def _outproj_kernel(ma_ref, mb_ref, mc_ref, x_ref, w_ref, gpost_ref, o_ref, y_scr):
    ma = jnp.concatenate([ma_ref[hd] for hd in range(A_HEADS)], axis=1)
    mb = jnp.concatenate([mb_ref[hd] for hd in range(B_HEADS)], axis=1)
    tm, d = o_ref.shape
    n_chunks = 4
    cols = d // n_chunks
    ss = jnp.zeros((tm, 1), F32)
    for c in range(n_chunks):
        cs = slice(c * cols, (c + 1) * cols)
        y = jnp.dot(ma, w_ref[0:A_Q, cs], preferred_element_type=F32)
        y += jnp.dot(mb, w_ref[A_Q:A_Q + B_G, cs], preferred_element_type=F32)
        y += jnp.dot(mc_ref[...], w_ref[A_Q + B_G:D_MIX, cs], preferred_element_type=F32)
        y_scr[:, cs] = y
        ss += jnp.sum(y * y, axis=-1, keepdims=True)
    inv_rms = lax.rsqrt(ss * (1.0 / d) + NORM_EPS)
    o_ref[...] = x_ref[...] + y_scr[...] * inv_rms * gpost_ref[...]


def _outproj(ma, mb, mc, x2d, w, gpost, tm):
    t, d = x2d.shape
    row = lambda i: (i, 0)
    fix = lambda i: (0, 0)
    return pl.pallas_call(
        _outproj_kernel,
        grid=(t // tm,),
        in_specs=[pl.BlockSpec((A_HEADS, tm, HEAD_DIM), lambda i: (0, i, 0)),
                  pl.BlockSpec((B_HEADS, tm, B_V), lambda i: (0, i, 0)),
                  pl.BlockSpec((tm, C_W), row), pl.BlockSpec((tm, d), row),
                  pl.BlockSpec(w.shape, fix, pipeline_mode=pl.Buffered(1)),
                  pl.BlockSpec(gpost.shape, fix, pipeline_mode=pl.Buffered(1))],
        out_specs=pl.BlockSpec((tm, d), row),
        out_shape=jax.ShapeDtypeStruct((t, d), F32),
        scratch_shapes=[pltpu.VMEM((tm, d), F32)],
        compiler_params=pltpu.CompilerParams(dimension_semantics=("parallel",),
                                             vmem_limit_bytes=VMEM_LIMIT_BYTES),
        name="outproj",
    )(ma, mb, mc, x2d, w, gpost)


def _rope_tables(seq):
    t = np.arange(seq)
    pos = np.stack([t // GRID_W, t % GRID_W], axis=0).astype(np.float64)

    def tables(n, lanes_used):
        j = np.arange(128)
        axis = np.minimum(j // n, 1)
        i = j % (n // 2)
        inv_freq = 1.0 / (ROPE_THETA ** (np.arange(0, n, 2, dtype=np.float64) / n))
        ang = pos[axis, :].T * inv_freq[i][None, :]
        used = (j < lanes_used)[None, :]
        low = ((j % n) < n // 2)[None, :]
        cos = np.where(used, np.cos(ang), 0.0)
        sin = np.where(used, np.sin(ang), 0.0)
        return tuple(jnp.asarray(a, F32) for a in (cos, np.where(low, -sin, 0.0), np.where(low, 0.0, sin)))

    return tables(HEAD_DIM // 2, 128) + tables(B_ROPE // 2, B_ROPE)


def _w_in_repack_kernel(w_ref, o_ref):
    kr_lo = OFF_BG
    kr_hi = kr_lo + B_ROPE
    tc = w_ref.shape[1]
    chunk = 256
    for lo in range(0, kr_lo, chunk):
        o_ref[:, lo:lo + chunk] = w_ref[lo:lo + chunk, :].T.astype(BF16)
    for lo in range(kr_lo, OFF_BKR, chunk):
        o_ref[:, lo:lo + chunk] = w_ref[lo + B_ROPE:lo + B_ROPE + chunk, :].T.astype(BF16)
    tail = jnp.concatenate([w_ref[kr_lo:kr_hi, :], jnp.zeros((128 - B_ROPE, tc), F32)], axis=0)
    o_ref[:, OFF_BKR:W_IN_COLS] = tail.T.astype(BF16)


def _prep_w_in(w, tc=256):
    depth, d, n = w.shape
    wt = jnp.swapaxes(w, 1, 2)
    return pl.pallas_call(
        _w_in_repack_kernel,
        grid=(depth, d // tc),
        in_specs=[pl.BlockSpec((None, n, tc), lambda l, i: (l, 0, i))],
        out_specs=pl.BlockSpec((None, tc, W_IN_COLS), lambda l, i: (l, i, 0)),
        out_shape=jax.ShapeDtypeStruct((depth, d, W_IN_COLS), BF16),
        compiler_params=pltpu.CompilerParams(dimension_semantics=("parallel", "parallel"),
                                             vmem_limit_bytes=VMEM_LIMIT_BYTES),
        name="w_in_repack",
    )(wt)


def _prep_w_uq(w):
    r = w.shape[0]
    w = w.reshape(r, B_HEADS, B_NOPE + B_ROPE)
    w = jnp.pad(w, ((0, 0), (0, 0), (0, B_QK_PAD - B_NOPE - B_ROPE)))
    return w.reshape(r, B_HEADS * B_QK_PAD).astype(BF16)


def _prep_w_ukv(w):
    r = w.shape[0]
    w = w.reshape(r, B_HEADS, B_NOPE + B_V)
    wk = w[:, :, :B_NOPE].reshape(r, B_HEADS * B_NOPE)
    wv = w[:, :, B_NOPE:].reshape(r, B_HEADS * B_V)
    return wk.astype(BF16), wv.astype(BF16)


def kernel(x, norm_pre, norm_post, w_in, a_q_norm, a_k_norm, b_q_norm, b_kv_norm, b_w_uq, b_w_ukv, c_rpb, w_out):
    batch, seq, d = x.shape
    depth = w_in.shape[0]
    assert seq % C_TQ == 0 and seq // GRID_W >= C_WROWS and w_in.shape[2] == W_IN_COLS - 64
    tabs = _rope_tables(seq)
    bias, pat_ids = _nbr_bias(c_rpb, seq)
    w_in_all = _prep_w_in(w_in)
    h = x.reshape(batch * seq, d)
    for l in range(depth):
        wuk, wuv = _prep_w_ukv(b_w_ukv[l])
        qa, ka, va, ga, qb, kb, vb, gb, qc, kc, vc, gc = _inproj(
            h, seq, l, norm_pre[l][None], w_in_all, _prep_w_uq(b_w_uq[l]), wuk, wuv,
            a_q_norm[l][None], a_k_norm[l][None], b_q_norm[l][None], b_kv_norm[l][None],
            tabs, tm=256)
        mix_a = _dense_attn(qa, ka, va, ga, seq, dv=HEAD_DIM, tq=256, name="attn_a")
        mix_b = _dense_attn(qb, kb, vb, gb, seq, dv=B_V, tq=256, name="attn_b")
        mix_c = _nbr_attn(qc, kc, vc, bias, pat_ids, l, gc, seq)
        h = _outproj(mix_a, mix_b, mix_c, h, w_out[l].astype(BF16), norm_post[l][None], tm=512)
    return h.reshape(batch, seq, d)
```

```python
import functools

import jax
import jax.numpy as jnp
import numpy as np
from jax import lax
from jax.experimental import pallas as pl
from jax.experimental.pallas import tpu as pltpu

F32 = jnp.float32
BF16 = jnp.bfloat16

GRID_W = 64
HEAD_DIM = 128
A_HEADS = 8
A_KV_HEADS = 2
A_GROUP = A_HEADS // A_KV_HEADS
B_HEADS = 4
B_Q_LORA = 512
B_KV_LORA = 256
B_NOPE = 128
B_ROPE = 64
B_V = 128
B_QK_PAD = 256
C_HEADS = 4
WIN_ROWS = 8
WIN_COLS = 16
ROPE_THETA = 10000.0
NORM_EPS = 1e-6
NEG_BIG = -1e30
LOG2E = 1.4426950408889634

A_Q = A_HEADS * HEAD_DIM
A_KV = A_KV_HEADS * HEAD_DIM
B_G = B_HEADS * B_V
C_W = C_HEADS * HEAD_DIM
D_MIX = A_Q + B_G + C_W

OFF_AQ = 0
OFF_AK = OFF_AQ + A_Q
OFF_AV = OFF_AK + A_KV
OFF_AG = OFF_AV + A_KV
OFF_BCQ = OFF_AG + A_Q
OFF_BCKV = OFF_BCQ + B_Q_LORA
OFF_BG = OFF_BCKV + B_KV_LORA
OFF_CQ = OFF_BG + B_G
OFF_CK = OFF_CQ + C_W
OFF_CV = OFF_CK + C_W
OFF_CG = OFF_CV + C_W
OFF_BKR = OFF_CG + C_W
W_IN_COLS = OFF_BKR + 128

C_QROWS = 4
C_WROWS = 12
C_TQ = C_QROWS * GRID_W
C_TK = C_WROWS * GRID_W

VMEM_LIMIT_BYTES = 56 * 1024 * 1024


def _rms(x, gain):
    return x * lax.rsqrt(jnp.mean(x * x, axis=-1, keepdims=True) + NORM_EPS) * gain


def _rope(x, cos, sin_lo, sin_hi, half):
    up = pltpu.roll(x, 128 - half, 1)
    dn = pltpu.roll(x, half, 1)
    return x * cos + up * sin_lo + dn * sin_hi


def _silu(g):
    return g * (1.0 / (1.0 + jnp.exp(-g)))


_NT = (((1,), (1,)), ((), ()))


def _inproj_kernel(x_ref, gpre_ref, w_ref, wuq_ref, wuk_ref, wuv_ref,
                   aqn_ref, akn_ref, bqn_ref, bkvn_ref,
                   cos_a, slo_a, shi_a, cos_b, slo_b, shi_b,
                   qa_ref, ka_ref, va_ref, ga_ref, qb_ref, kb_ref, vb_ref, gb_ref,
                   qc_ref, kc_ref, vc_ref, gc_ref, h_scr):
    x = x_ref[...]
    h_scr[...] = (x * gpre_ref[...]).astype(BF16)
    inv_rms = lax.rsqrt(jnp.mean(x * x, axis=-1, keepdims=True) + NORM_EPS)
    tm = x.shape[0]

    def proj(off, width):
        return jnp.dot(h_scr[...], w_ref[:, off:off + width], preferred_element_type=F32) * inv_rms

    ones = jnp.ones((tm, HEAD_DIM), BF16)

    ca, sla, sha = cos_a[...], slo_a[...], shi_a[...]
    cb, slb, shb = cos_b[...], slo_b[...], shi_b[...]
    scale_a = HEAD_DIM ** -0.5 * LOG2E
    scale_b = (B_NOPE + B_ROPE) ** -0.5 * LOG2E

    cq = _rms(proj(OFF_BCQ, B_Q_LORA), bqn_ref[...]).astype(BF16)
    ckv = _rms(proj(OFF_BCKV, B_KV_LORA), bkvn_ref[...]).astype(BF16)
    kpe = _rope(proj(OFF_BKR, 128), cb, slb, shb, 16).astype(BF16)

    pq = proj(OFF_AQ, A_Q)
    for hd in range(A_HEADS):
        sl = slice(hd * HEAD_DIM, (hd + 1) * HEAD_DIM)
        q = _rope(_rms(pq[:, sl], aqn_ref[...]), ca, sla, sha, 32)
        qa_ref[hd] = (q * scale_a).astype(BF16)
    pk = proj(OFF_AK, A_KV)
    pv = proj(OFF_AV, A_KV)
    for hd in range(A_KV_HEADS):
        sl = slice(hd * HEAD_DIM, (hd + 1) * HEAD_DIM)
        ka_ref[hd] = _rope(_rms(pk[:, sl], akn_ref[...]), ca, sla, sha, 32).astype(BF16)
        va_ref[hd, :, 0:HEAD_DIM] = pv[:, sl].astype(BF16)
        va_ref[hd, :, HEAD_DIM:2 * HEAD_DIM] = ones
    ga = _silu(proj(OFF_AG, A_Q)).astype(BF16)
    for hd in range(A_HEADS):
        ga_ref[hd] = ga[:, hd * HEAD_DIM:(hd + 1) * HEAD_DIM]

    qb = jnp.dot(cq, wuq_ref[...], preferred_element_type=F32)
    kn = jnp.dot(ckv, wuk_ref[...], preferred_element_type=F32)
    vb = jnp.dot(ckv, wuv_ref[...], preferred_element_type=F32)
    gb = _silu(proj(OFF_BG, B_G)).astype(BF16)
    for hd in range(B_HEADS):
        lo = hd * B_QK_PAD
        qb_ref[hd, :, 0:B_NOPE] = (qb[:, lo:lo + B_NOPE] * scale_b).astype(BF16)
        qpe = _rope(qb[:, lo + B_NOPE:lo + B_QK_PAD], cb, slb, shb, 16)
        qb_ref[hd, :, B_NOPE:B_QK_PAD] = (qpe * scale_b).astype(BF16)
        kb_ref[hd, :, 0:B_NOPE] = kn[:, hd * B_NOPE:(hd + 1) * B_NOPE].astype(BF16)
        kb_ref[hd, :, B_NOPE:B_QK_PAD] = kpe
        vb_ref[hd, :, 0:B_V] = vb[:, hd * B_V:(hd + 1) * B_V].astype(BF16)
        vb_ref[hd, :, B_V:2 * B_V] = ones
        gb_ref[hd] = gb[:, hd * B_V:(hd + 1) * B_V]

    gc_ref[...] = _silu(proj(OFF_CG, C_W)).astype(BF16)
    qc_ref[...] = (proj(OFF_CQ, C_W) * scale_a).astype(BF16)
    pvc = proj(OFF_CV, C_W)
    for hd in range(C_HEADS):
        vc_ref[:, 2 * hd * HEAD_DIM:(2 * hd + 1) * HEAD_DIM] = (
            pvc[:, hd * HEAD_DIM:(hd + 1) * HEAD_DIM].astype(BF16))
        vc_ref[:, (2 * hd + 1) * HEAD_DIM:(2 * hd + 2) * HEAD_DIM] = ones
    kc_ref[...] = proj(OFF_CK, C_W).astype(BF16)


def _inproj(x2d, seq, layer, gpre, w_all, wuq, wuk, wuv, aqn, akn, bqn, bkvn, tabs, tm):
    t, d = x2d.shape
    n_s = seq // tm
    row = lambda i: (i, 0)
    fix = lambda i: (0, 0)
    tab = lambda i: (i % n_s, 0)

    def resident(a):
        return pl.BlockSpec(a.shape, fix, pipeline_mode=pl.Buffered(1))

    w_spec = pl.BlockSpec((None,) + w_all.shape[1:], lambda i: (layer, 0, 0),
                          pipeline_mode=pl.Buffered(1))

    head_major = ((A_HEADS, HEAD_DIM), (A_KV_HEADS, HEAD_DIM), (A_KV_HEADS, 2 * HEAD_DIM),
                  (A_HEADS, HEAD_DIM), (B_HEADS, B_QK_PAD), (B_HEADS, B_QK_PAD),
                  (B_HEADS, 2 * B_V), (B_HEADS, B_V))
    token_major = (C_W, C_W, 2 * C_W, C_W)
    out_specs = ([pl.BlockSpec((nh, tm, n), lambda i: (0, i, 0)) for nh, n in head_major]
                 + [pl.BlockSpec((tm, n), row) for n in token_major])
    out_shape = ([jax.ShapeDtypeStruct((nh, t, n), BF16) for nh, n in head_major]
                 + [jax.ShapeDtypeStruct((t, n), BF16) for n in token_major])
    return pl.pallas_call(
        _inproj_kernel,
        grid=(t // tm,),
        in_specs=[pl.BlockSpec((tm, d), row), resident(gpre), w_spec, resident(wuq),
                  resident(wuk), resident(wuv), resident(aqn), resident(akn), resident(bqn),
                  resident(bkvn)] + [pl.BlockSpec((tm, 128), tab)] * 6,
        out_specs=out_specs,
        out_shape=out_shape,
        scratch_shapes=[pltpu.VMEM((tm, d), BF16)],
        compiler_params=pltpu.CompilerParams(dimension_semantics=("parallel",),
                                             vmem_limit_bytes=VMEM_LIMIT_BYTES),
        name="inproj",
    )(x2d, gpre, w_all, wuq, wuk, wuv, aqn, akn, bqn, bkvn, *tabs)


def _softmax_pv(s, v_ext, dv):
    m = jnp.max(s, axis=-1, keepdims=True)
    p = jnp.exp2(s - m).astype(BF16)
    oe = jnp.dot(p, v_ext, preferred_element_type=F32)
    return oe[:, :dv] * (1.0 / oe[:, dv:])


def _dense_attn_kernel(q_ref, k_ref, v_ref, g_ref, o_ref, s_a, s_b, *, shared_kv, dv, tq):
    heads, seq, _ = q_ref.shape
    n_t = seq // tq
    bufs = (s_a, s_b)

    def qk(hd, u, s_ref):
        k = k_ref[0] if shared_kv else k_ref[hd]
        s_ref[...] = lax.dot_general(q_ref[hd, u * tq:(u + 1) * tq, :], k, _NT,
                                     preferred_element_type=F32)

    def finish(hd, u, s_ref):
        v = v_ref[0] if shared_kv else v_ref[hd]
        o = _softmax_pv(s_ref[...], v, dv)
        rows = slice(u * tq, (u + 1) * tq)
        o_ref[hd, rows, :] = (o * g_ref[hd, rows, :].astype(F32)).astype(BF16)

    def body(hd, carry):
        nxt = jnp.minimum(hd + 1, heads - 1)
        for u in range(n_t):
            if u + 1 < n_t:
                qk(hd, u + 1, bufs[(u + 1) % 2])
            else:
                qk(nxt, 0, bufs[0])
            finish(hd, u, bufs[u % 2])
        return carry

    qk(0, 0, s_a)
    lax.fori_loop(0, heads, body, 0)


def _dense_attn(q, k, v_ext, gate, seq, *, tq, name):
    heads, t, dq = q.shape
    kv_heads = k.shape[0]
    dv = gate.shape[2]
    shared_kv = kv_heads < heads
    assert (seq // tq) % 2 == 0 and v_ext.shape[2] == 2 * dv
    per_step = heads // kv_heads if shared_kv else heads
    kv_per_step = 1 if shared_kv else per_step
    blk = lambda b, h: (h, b, 0)
    kern = functools.partial(_dense_attn_kernel, shared_kv=shared_kv, dv=dv, tq=tq)
    return pl.pallas_call(
        kern,
        grid=(t // seq, heads // per_step),
        in_specs=[pl.BlockSpec((per_step, seq, dq), blk),
                  pl.BlockSpec((kv_per_step, seq, dq), blk),
                  pl.BlockSpec((kv_per_step, seq, 2 * dv), blk),
                  pl.BlockSpec((per_step, seq, dv), blk)],
        out_specs=pl.BlockSpec((per_step, seq, dv), blk),
        out_shape=jax.ShapeDtypeStruct((heads, t, dv), BF16),
        scratch_shapes=[pltpu.VMEM((tq, seq), F32), pltpu.VMEM((tq, seq), F32)],
        compiler_params=pltpu.CompilerParams(dimension_semantics=("parallel", "parallel"),
                                             vmem_limit_bytes=VMEM_LIMIT_BYTES),
        name=name,
    )(q, k, v_ext, gate)


def _nbr_attn_kernel(pid_ref, q_ref, qn_ref, k_ref, kn_ref, v_ref, bias_ref, g_ref,
                     o_ref, s_a, s_b, *, max_ws, nb):
    j = pl.program_id(0)
    i0 = (2 * j) % nb

    def window_start(i):
        ws = jnp.clip(C_QROWS * i - WIN_ROWS // 2, 0, max_ws)
        return pl.multiple_of(ws * GRID_W, GRID_W)

    def qk(q, keys_ref, i, s_ref, hd):
        sl = slice(hd * HEAD_DIM, (hd + 1) * HEAD_DIM)
        s_ref[hd] = lax.dot_general(q[:, sl], keys_ref[pl.ds(window_start(i), C_TK), sl], _NT,
                                    preferred_element_type=F32)

    def finish(s_ref, i, rows, hd):
        sl = slice(hd * HEAD_DIM, (hd + 1) * HEAD_DIM)
        vw = v_ref[pl.ds(window_start(i), C_TK), 2 * hd * HEAD_DIM:2 * (hd + 1) * HEAD_DIM]
        s = s_ref[hd] + bias_ref[pid_ref[i], hd]
        o = _softmax_pv(s, vw, HEAD_DIM) * g_ref[rows, sl].astype(F32)
        o_ref[rows, sl] = o.astype(BF16)

    first, second = slice(0, C_TQ), slice(C_TQ, 2 * C_TQ)

    @pl.when(j == 0)
    def _():
        for hd in range(C_HEADS):
            qk(q_ref[first, :], k_ref, i0, s_a, hd)

    for hd in range(C_HEADS):
        qk(q_ref[second, :], k_ref, i0 + 1, s_b, hd)
        finish(s_a, i0, first, hd)
    for hd in range(C_HEADS):
        qk(qn_ref[...], kn_ref, (i0 + 2) % nb, s_a, hd)
        finish(s_b, i0 + 1, second, hd)


def _nbr_attn(q, k, v, bias, pat_ids, layer, gate, seq):
    t = q.shape[0]
    nb = seq // C_TQ
    n_blocks = t // C_TQ
    assert nb % 2 == 0
    kern = functools.partial(_nbr_attn_kernel, max_ws=seq // GRID_W - C_WROWS, nb=nb)
    nxt = lambda j: jnp.minimum(2 * j + 2, n_blocks - 1)
    n_pat = bias.shape[0]
    bias_spec = pl.BlockSpec((n_pat, None, C_HEADS, C_TQ, C_TK),
                             lambda j, pid: (0, layer, 0, 0, 0), pipeline_mode=pl.Buffered(1))
    grid_spec = pltpu.PrefetchScalarGridSpec(
        num_scalar_prefetch=1,
        grid=(n_blocks // 2,),
        in_specs=[pl.BlockSpec((2 * C_TQ, C_W), lambda j, pid: (j, 0)),
                  pl.BlockSpec((C_TQ, C_W), lambda j, pid: (nxt(j), 0)),
                  pl.BlockSpec((seq, C_W), lambda j, pid: (2 * j // nb, 0)),
                  pl.BlockSpec((seq, C_W), lambda j, pid: (nxt(j) // nb, 0)),
                  pl.BlockSpec((seq, 2 * C_W), lambda j, pid: (2 * j // nb, 0)),
                  bias_spec,
                  pl.BlockSpec((2 * C_TQ, C_W), lambda j, pid: (j, 0))],
        out_specs=pl.BlockSpec((2 * C_TQ, C_W), lambda j, pid: (j, 0)),
        scratch_shapes=[pltpu.VMEM((C_HEADS, C_TQ, C_TK), F32)] * 2)
    return pl.pallas_call(
        kern,
        grid_spec=grid_spec,
        out_shape=jax.ShapeDtypeStruct((t, C_W), BF16),
        compiler_params=pltpu.CompilerParams(dimension_semantics=("arbitrary",),
                                             vmem_limit_bytes=VMEM_LIMIT_BYTES),
        name="attn_c",
    )(pat_ids, q, q, k, k, v, bias, gate)


def _nbr_bias(rpb, seq):
    depth, heads, n_dr, n_dc = rpb.shape
    g = depth * heads
    rows = seq // GRID_W
    nb = seq // C_TQ
    neg = NEG_BIG * LOG2E
    lead = GRID_W - WIN_COLS
    u = jnp.pad(rpb.reshape(g, n_dr, n_dc) * LOG2E,
                ((0, 0), (0, 1), (lead, 2 * GRID_W - n_dc - lead)))
    blk = np.arange(nb)[:, None, None]
    r = C_QROWS * blk + np.arange(C_QROWS)[None, :, None]
    ws = np.clip(C_QROWS * blk - WIN_ROWS // 2, 0, rows - C_WROWS)
    kr = ws + np.arange(C_WROWS)[None, None, :]
    r0 = np.clip(r - WIN_ROWS // 2, 0, rows - WIN_ROWS)
    valid_r = (kr >= r0) & (kr < r0 + WIN_ROWS)
    slab_id = np.where(valid_r, kr - r + (WIN_ROWS - 1), -1)
    pats, pat_ids = np.unique(slab_id.reshape(nb, -1), axis=0, return_inverse=True)
    pats = np.where(pats < 0, n_dr, pats).reshape(-1, C_QROWS, C_WROWS)
    n_pat = pats.shape[0]
    table = tuple(tuple(tuple(int(s) for s in row) for row in pat) for pat in pats)
    bias = pl.pallas_call(
        functools.partial(_nbr_bias_kernel, table=table, n_dr=n_dr, neg=neg),
        grid=(g,),
        in_specs=[pl.BlockSpec((None, n_dr + 1, 2 * GRID_W), lambda i: (i, 0, 0))],
        out_specs=pl.BlockSpec((n_pat, None, C_TQ, C_TK), lambda i: (0, i, 0, 0)),
        out_shape=jax.ShapeDtypeStruct((n_pat, g, C_TQ, C_TK), F32),
        scratch_shapes=[pltpu.VMEM((n_dr + 1, GRID_W, 2 * GRID_W), F32)],
        compiler_params=pltpu.CompilerParams(dimension_semantics=("parallel",)),
        name="nbr_bias",
    )(u)
    return (bias.reshape(n_pat, depth, heads, C_TQ, C_TK),
            jnp.asarray(pat_ids.reshape(nb), jnp.int32))


def _nbr_bias_kernel(u_ref, o_ref, slab_scr, *, table, n_dr, neg):
    shape = (GRID_W, 2 * GRID_W)
    lane = lax.broadcasted_iota(jnp.int32, shape, 1)
    qc = lax.broadcasted_iota(jnp.int32, shape, 0)
    kc = lane % GRID_W
    left = lane < GRID_W
    c0 = jnp.clip(qc - WIN_COLS // 2, 0, GRID_W - WIN_COLS)
    valid = (kc >= c0) & (kc < c0 + WIN_COLS)
    for dr in range(n_dr):
        rows = jnp.broadcast_to(u_ref[dr:dr + 1, :], shape)
        skew = pltpu.roll(rows, GRID_W + 1, 1, stride=1, stride_axis=0)
        both = jnp.where(left, skew, pltpu.roll(skew, GRID_W, 1))
        slab_scr[dr] = jnp.where(valid, both, neg)
    slab_scr[n_dr] = jnp.full(shape, neg, F32)
    for p, pat in enumerate(table):
        for q, row in enumerate(pat):
            for kp in range(len(row) // 2):
                tile = jnp.where(left, slab_scr[row[2 * kp]], slab_scr[row[2 * kp + 1]])
                o_ref[p, q * GRID_W:(q + 1) * GRID_W, 2 * kp * GRID_W:2 * (kp + 1) * GRID_W] = tile


def _outproj_kernel(ma_ref, mb_ref, mc_ref, x_ref, w_ref, gpost_ref, o_ref, y_a, y_b, *, n_tiles):
    i = pl.program_id(0)
    tm, d = o_ref.shape
    n_chunks = 4
    cols, rows = d // n_chunks, tm // n_chunks

    def dots(y_ref, c, after):
        heads_a = [ma_ref[hd] for hd in range(A_HEADS)]
        if after is not None:
            zero = pltpu.bitcast(after, F32)
            top = heads_a[0][0:16, :].astype(F32) + jnp.concatenate([zero, zero], axis=0)
            heads_a[0] = jnp.concatenate([top.astype(BF16), heads_a[0][16:, :]], axis=0)
        ma = jnp.concatenate(heads_a, axis=1)
        mb = jnp.concatenate([mb_ref[hd] for hd in range(B_HEADS)], axis=1)
        cs = slice(c * cols, (c + 1) * cols)
        y = jnp.dot(ma, w_ref[0:A_Q, cs], preferred_element_type=F32)
        y += jnp.dot(mb, w_ref[A_Q:A_Q + B_G, cs], preferred_element_type=F32)
        y += jnp.dot(mc_ref[...], w_ref[A_Q + B_G:D_MIX, cs], preferred_element_type=F32)
        y_ref[:, cs] = y

    def epilogue(y_ref, c):
        rs = slice(c * rows, (c + 1) * rows)
        out = x_ref[rs, :] + _rms(y_ref[rs, :], gpost_ref[...])
        o_ref[rs, :] = out
        bits = pltpu.bitcast(out, jnp.uint32)
        acc = None
        for r in range(0, rows, 8):
            for l in range(0, d, 128):
                word = bits[r:r + 8, l:l + 128]
                acc = word if acc is None else acc | word
        return (acc >> 16) >> 16

    def step(y_cur, y_prev):
        after = None
        for c in range(n_chunks):
            if y_cur is not None:
                dots(y_cur, c, after)
            if y_prev is not None:
                after = epilogue(y_prev, c)

    inner = (i > 0) & (i < n_tiles)

    @pl.when(i == 0)
    def _():
        step(y_a, None)

    @pl.when(inner & (i % 2 == 1))
    def _():
        step(y_b, y_a)

    @pl.when(inner & (i % 2 == 0))
    def _():
        step(y_a, y_b)

    @pl.when(i == n_tiles)
    def _():
        step(None, y_a if (n_tiles - 1) % 2 == 0 else y_b)


def _outproj(ma, mb, mc, x2d, w, gpost, tm):
    t, d = x2d.shape
    n_tiles = t // tm
    cur = lambda i: (jnp.minimum(i, n_tiles - 1), 0)
    cur_hm = lambda i: (0, jnp.minimum(i, n_tiles - 1), 0)
    prev = lambda i: (jnp.maximum(i - 1, 0), 0)
    fix = lambda i: (0, 0)
    return pl.pallas_call(
        functools.partial(_outproj_kernel, n_tiles=n_tiles),
        grid=(n_tiles + 1,),
        in_specs=[pl.BlockSpec((A_HEADS, tm, HEAD_DIM), cur_hm),
                  pl.BlockSpec((B_HEADS, tm, B_V), cur_hm),
                  pl.BlockSpec((tm, C_W), cur), pl.BlockSpec((tm, d), prev),
                  pl.BlockSpec(w.shape, fix, pipeline_mode=pl.Buffered(1)),
                  pl.BlockSpec(gpost.shape, fix, pipeline_mode=pl.Buffered(1))],
        out_specs=pl.BlockSpec((tm, d), prev),
        out_shape=jax.ShapeDtypeStruct((t, d), F32),
        scratch_shapes=[pltpu.VMEM((tm, d), F32), pltpu.VMEM((tm, d), F32)],
        compiler_params=pltpu.CompilerParams(dimension_semantics=("arbitrary",),
                                             vmem_limit_bytes=VMEM_LIMIT_BYTES),
        name="outproj",
    )(ma, mb, mc, x2d, w, gpost)


def _rope_tables(seq):
    t = np.arange(seq)
    pos = np.stack([t // GRID_W, t % GRID_W], axis=0).astype(np.float64)

    def tables(n, lanes_used):
        j = np.arange(128)
        axis = np.minimum(j // n, 1)
        i = j % (n // 2)
        inv_freq = 1.0 / (ROPE_THETA ** (np.arange(0, n, 2, dtype=np.float64) / n))
        ang = pos[axis, :].T * inv_freq[i][None, :]
        used = (j < lanes_used)[None, :]
        low = ((j % n) < n // 2)[None, :]
        cos = np.where(used, np.cos(ang), 0.0)
        sin = np.where(used, np.sin(ang), 0.0)
        return tuple(jnp.asarray(a, F32) for a in (cos, np.where(low, -sin, 0.0), np.where(low, 0.0, sin)))

    return tables(HEAD_DIM // 2, 128) + tables(B_ROPE // 2, B_ROPE)


def _w_in_repack_kernel(w_ref, o_ref):
    kr_lo = OFF_BG
    kr_hi = kr_lo + B_ROPE
    tc = w_ref.shape[1]
    chunk = 256
    for lo in range(0, kr_lo, chunk):
        o_ref[:, lo:lo + chunk] = w_ref[lo:lo + chunk, :].T.astype(BF16)
    for lo in range(kr_lo, OFF_BKR, chunk):
        o_ref[:, lo:lo + chunk] = w_ref[lo + B_ROPE:lo + B_ROPE + chunk, :].T.astype(BF16)
    tail = jnp.concatenate([w_ref[kr_lo:kr_hi, :], jnp.zeros((128 - B_ROPE, tc), F32)], axis=0)
    o_ref[:, OFF_BKR:W_IN_COLS] = tail.T.astype(BF16)


def _prep_w_in(w, tc=256):
    depth, d, n = w.shape
    wt = jnp.swapaxes(w, 1, 2)
    return pl.pallas_call(
        _w_in_repack_kernel,
        grid=(depth, d // tc),
        in_specs=[pl.BlockSpec((None, n, tc), lambda l, i: (l, 0, i))],
        out_specs=pl.BlockSpec((None, tc, W_IN_COLS), lambda l, i: (l, i, 0)),
        out_shape=jax.ShapeDtypeStruct((depth, d, W_IN_COLS), BF16),
        compiler_params=pltpu.CompilerParams(dimension_semantics=("parallel", "parallel"),
                                             vmem_limit_bytes=VMEM_LIMIT_BYTES),
        name="w_in_repack",
    )(wt)


def _prep_w_uq(w):
    r = w.shape[0]
    w = w.reshape(r, B_HEADS, B_NOPE + B_ROPE)
    w = jnp.pad(w, ((0, 0), (0, 0), (0, B_QK_PAD - B_NOPE - B_ROPE)))
    return w.reshape(r, B_HEADS * B_QK_PAD).astype(BF16)


def _prep_w_ukv(w):
    r = w.shape[0]
    w = w.reshape(r, B_HEADS, B_NOPE + B_V)
    wk = w[:, :, :B_NOPE].reshape(r, B_HEADS * B_NOPE)
    wv = w[:, :, B_NOPE:].reshape(r, B_HEADS * B_V)
    return wk.astype(BF16), wv.astype(BF16)


def kernel(x, norm_pre, norm_post, w_in, a_q_norm, a_k_norm, b_q_norm, b_kv_norm, b_w_uq, b_w_ukv, c_rpb, w_out):
    batch, seq, d = x.shape
    depth = w_in.shape[0]
    assert seq % C_TQ == 0 and seq // GRID_W >= C_WROWS and w_in.shape[2] == W_IN_COLS - 64
    tabs = _rope_tables(seq)
    bias, pat_ids = _nbr_bias(c_rpb, seq)
    w_in_all = _prep_w_in(w_in)
    h = x.reshape(batch * seq, d)
    for l in range(depth):
        wuk, wuv = _prep_w_ukv(b_w_ukv[l])
        qa, ka, va, ga, qb, kb, vb, gb, qc, kc, vc, gc = _inproj(
            h, seq, l, norm_pre[l][None], w_in_all, _prep_w_uq(b_w_uq[l]), wuk, wuv,
            a_q_norm[l][None], a_k_norm[l][None], b_q_norm[l][None], b_kv_norm[l][None],
            tabs, tm=256)
        mix_a = _dense_attn(qa, ka, va, ga, seq, tq=256, name="attn_a")
        mix_b = _dense_attn(qb, kb, vb, gb, seq, tq=256, name="attn_b")
        mix_c = _nbr_attn(qc, kc, vc, bias, pat_ids, l, gc, seq)
        h = _outproj(mix_a, mix_b, mix_c, h, w_out[l].astype(BF16), norm_post[l][None], tm=512)
    return h.reshape(batch, seq, d)
```

```python
import functools

import jax
import jax.numpy as jnp
import numpy as np
from jax import lax
from jax.experimental import pallas as pl
from jax.experimental.pallas import tpu as pltpu

F32 = jnp.float32
BF16 = jnp.bfloat16
LANES = 128
SUBLANES = 8

GRID_W = 64
HEAD_DIM = 128
A_HEADS = 8
A_KV_HEADS = 2
B_HEADS = 4
B_Q_LORA = 512
B_KV_LORA = 256
B_NOPE = 128
B_ROPE = 64
B_V = 128
B_QK_PAD = 256
C_HEADS = 4
WIN_ROWS = 8
WIN_COLS = 16
ROPE_THETA = 10000.0
NORM_EPS = 1e-6
NEG_BIG = -1e30
LOG2E = 1.4426950408889634

A_Q = A_HEADS * HEAD_DIM
A_KV = A_KV_HEADS * HEAD_DIM
B_G = B_HEADS * B_V
C_W = C_HEADS * HEAD_DIM
D_MIX = A_Q + B_G + C_W

OFF_AQ = 0
OFF_AK = OFF_AQ + A_Q
OFF_AV = OFF_AK + A_KV
OFF_AG = OFF_AV + A_KV
OFF_BCQ = OFF_AG + A_Q
OFF_BCKV = OFF_BCQ + B_Q_LORA
OFF_BG = OFF_BCKV + B_KV_LORA
OFF_CQ = OFF_BG + B_G
OFF_CK = OFF_CQ + C_W
OFF_CV = OFF_CK + C_W
OFF_CG = OFF_CV + C_W
OFF_BKR = OFF_CG + C_W
W_IN_COLS = OFF_BKR + LANES

C_QROWS = 4
C_WROWS = 12
C_TQ = C_QROWS * GRID_W
C_TK = C_WROWS * GRID_W

VMEM_LIMIT_BYTES = 56 * 1024 * 1024

INPROJ_ROWS = 256
ATTN_QUERY_ROWS = 256
OUTPROJ_ROWS = 512
ATTN_HEADS_PER_ITER = 2


def _rms(x, gain):
    return x * lax.rsqrt(jnp.mean(x * x, axis=-1, keepdims=True) + NORM_EPS) * gain


def _rope(x, cos, sin_lo, sin_hi, half):
    up = pltpu.roll(x, LANES - half, 1)
    dn = pltpu.roll(x, half, 1)
    return x * cos + up * sin_lo + dn * sin_hi


def _silu(g):
    return g * (1.0 / (1.0 + jnp.exp(-g)))


_NT = (((1,), (1,)), ((), ()))


def _inproj_kernel(x_ref, gpre_ref, w_ref, wuq_ref, wuk_ref, wuv_ref,
                   aqn_ref, akn_ref, bqn_ref, bkvn_ref,
                   cos_a, slo_a, shi_a, cos_b, slo_b, shi_b,
                   qa_ref, ka_ref, va_ref, ga_ref, qb_ref, kb_ref, vb_ref, gb_ref,
                   qc_ref, kc_ref, vc_ref, gc_ref, h_scr):
    x = x_ref[...]
    h_scr[...] = (x * gpre_ref[...]).astype(BF16)
    inv_rms = lax.rsqrt(jnp.mean(x * x, axis=-1, keepdims=True) + NORM_EPS)
    tm = x.shape[0]

    def proj(off, width):
        return jnp.dot(h_scr[...], w_ref[:, off:off + width], preferred_element_type=F32) * inv_rms

    ones = jnp.ones((tm, HEAD_DIM), BF16)

    ca, sla, sha = cos_a[...], slo_a[...], shi_a[...]
    cb, slb, shb = cos_b[...], slo_b[...], shi_b[...]
    scale_a = HEAD_DIM ** -0.5 * LOG2E
    scale_b = (B_NOPE + B_ROPE) ** -0.5 * LOG2E

    cq = _rms(proj(OFF_BCQ, B_Q_LORA), bqn_ref[...]).astype(BF16)
    ckv = _rms(proj(OFF_BCKV, B_KV_LORA), bkvn_ref[...]).astype(BF16)
    kpe = _rope(proj(OFF_BKR, LANES), cb, slb, shb, 16).astype(BF16)

    pq = proj(OFF_AQ, A_Q)
    for hd in range(A_HEADS):
        sl = slice(hd * HEAD_DIM, (hd + 1) * HEAD_DIM)
        q = _rope(_rms(pq[:, sl], aqn_ref[...]), ca, sla, sha, 32)
        qa_ref[hd] = (q * scale_a).astype(BF16)
    pk = proj(OFF_AK, A_KV)
    pv = proj(OFF_AV, A_KV)
    for hd in range(A_KV_HEADS):
        sl = slice(hd * HEAD_DIM, (hd + 1) * HEAD_DIM)
        ka_ref[hd] = _rope(_rms(pk[:, sl], akn_ref[...]), ca, sla, sha, 32).astype(BF16)
        va_ref[hd, :, 0:HEAD_DIM] = pv[:, sl].astype(BF16)
        va_ref[hd, :, HEAD_DIM:2 * HEAD_DIM] = ones
    ga = _silu(proj(OFF_AG, A_Q)).astype(BF16)
    for hd in range(A_HEADS):
        ga_ref[hd] = ga[:, hd * HEAD_DIM:(hd + 1) * HEAD_DIM]

    qb = jnp.dot(cq, wuq_ref[...], preferred_element_type=F32)
    kn = jnp.dot(ckv, wuk_ref[...], preferred_element_type=F32)
    vb = jnp.dot(ckv, wuv_ref[...], preferred_element_type=F32)
    gb = _silu(proj(OFF_BG, B_G)).astype(BF16)
    for hd in range(B_HEADS):
        lo = hd * B_QK_PAD
        qb_ref[hd, :, 0:B_NOPE] = (qb[:, lo:lo + B_NOPE] * scale_b).astype(BF16)
        qpe = _rope(qb[:, lo + B_NOPE:lo + B_QK_PAD], cb, slb, shb, 16)
        qb_ref[hd, :, B_NOPE:B_QK_PAD] = (qpe * scale_b).astype(BF16)
        kb_ref[hd, :, 0:B_NOPE] = kn[:, hd * B_NOPE:(hd + 1) * B_NOPE].astype(BF16)
        kb_ref[hd, :, B_NOPE:B_QK_PAD] = kpe
        vb_ref[hd, :, 0:B_V] = vb[:, hd * B_V:(hd + 1) * B_V].astype(BF16)
        vb_ref[hd, :, B_V:2 * B_V] = ones
        gb_ref[hd] = gb[:, hd * B_V:(hd + 1) * B_V]

    gc_ref[...] = _silu(proj(OFF_CG, C_W)).astype(BF16)
    qc_ref[...] = (proj(OFF_CQ, C_W) * scale_a).astype(BF16)
    pvc = proj(OFF_CV, C_W)
    for hd in range(C_HEADS):
        vc_ref[:, 2 * hd * HEAD_DIM:(2 * hd + 1) * HEAD_DIM] = (
            pvc[:, hd * HEAD_DIM:(hd + 1) * HEAD_DIM].astype(BF16))
        vc_ref[:, (2 * hd + 1) * HEAD_DIM:(2 * hd + 2) * HEAD_DIM] = ones
    kc_ref[...] = proj(OFF_CK, C_W).astype(BF16)


def _inproj(x2d, seq, layer, gpre, w_all, wuq, wuk, wuv, aqn, akn, bqn, bkvn, tabs, tm):
    t, d = x2d.shape
    n_s = seq // tm
    row = lambda i: (i, 0)
    fix = lambda i: (0, 0)
    tab = lambda i: (i % n_s, 0)

    def resident(a):
        return pl.BlockSpec(a.shape, fix, pipeline_mode=pl.Buffered(1))

    w_spec = pl.BlockSpec((None,) + w_all.shape[1:], lambda i: (layer, 0, 0),
                          pipeline_mode=pl.Buffered(1))

    head_major = ((A_HEADS, HEAD_DIM), (A_KV_HEADS, HEAD_DIM), (A_KV_HEADS, 2 * HEAD_DIM),
                  (A_HEADS, HEAD_DIM), (B_HEADS, B_QK_PAD), (B_HEADS, B_QK_PAD),
                  (B_HEADS, 2 * B_V), (B_HEADS, B_V))
    token_major = (C_W, C_W, 2 * C_W, C_W)
    out_specs = ([pl.BlockSpec((nh, tm, n), lambda i: (0, i, 0)) for nh, n in head_major]
                 + [pl.BlockSpec((tm, n), row) for n in token_major])
    out_shape = ([jax.ShapeDtypeStruct((nh, t, n), BF16) for nh, n in head_major]
                 + [jax.ShapeDtypeStruct((t, n), BF16) for n in token_major])
    return pl.pallas_call(
        _inproj_kernel,
        grid=(t // tm,),
        in_specs=[pl.BlockSpec((tm, d), row), resident(gpre), w_spec, resident(wuq),
                  resident(wuk), resident(wuv), resident(aqn), resident(akn), resident(bqn),
                  resident(bkvn)] + [pl.BlockSpec((tm, LANES), tab)] * 6,
        out_specs=out_specs,
        out_shape=out_shape,
        scratch_shapes=[pltpu.VMEM((tm, d), BF16)],
        compiler_params=pltpu.CompilerParams(dimension_semantics=("parallel",),
                                             vmem_limit_bytes=VMEM_LIMIT_BYTES),
        name="inproj",
    )(x2d, gpre, w_all, wuq, wuk, wuv, aqn, akn, bqn, bkvn, *tabs)


def _softmax_pv(s, v_ext, dv):
    m = jnp.max(s, axis=-1, keepdims=True)
    p = jnp.exp2(s - m).astype(BF16)
    oe = jnp.dot(p, v_ext, preferred_element_type=F32)
    return oe[:, :dv] * (1.0 / oe[:, dv:])


def _dense_attn_kernel(q_ref, k_ref, v_ref, g_ref, o_ref, s_a, s_b, *, dv, tq, heads_per_iter):
    heads, seq, _ = q_ref.shape
    group = heads // k_ref.shape[0]
    n_t = seq // tq
    bufs = (s_a, s_b)

    def qk(hd, u, s_ref):
        s_ref[...] = lax.dot_general(q_ref[hd, u * tq:(u + 1) * tq, :], k_ref[hd // group], _NT,
                                     preferred_element_type=F32)

    def finish(hd, u, s_ref):
        o = _softmax_pv(s_ref[...], v_ref[hd // group], dv)
        rows = slice(u * tq, (u + 1) * tq)
        o_ref[hd, rows, :] = (o * g_ref[hd, rows, :].astype(F32)).astype(BF16)

    def body(it, carry):
        for h in range(heads_per_iter):
            hd = it * heads_per_iter + h
            nxt = jnp.minimum(hd + 1, heads - 1)
            for u in range(n_t):
                if u + 1 < n_t:
                    qk(hd, u + 1, bufs[(u + 1) % 2])
                else:
                    qk(nxt, 0, bufs[0])
                finish(hd, u, bufs[u % 2])
        return carry

    qk(0, 0, s_a)
    lax.fori_loop(0, heads // heads_per_iter, body, 0)


def _dense_attn(q, k, v_ext, gate, seq, *, tq, name):
    heads, t, dq = q.shape
    kv_heads = k.shape[0]
    dv = gate.shape[2]
    assert (seq // tq) % 2 == 0 and v_ext.shape[2] == 2 * dv
    assert heads % kv_heads == 0 and heads % ATTN_HEADS_PER_ITER == 0
    blk = lambda b: (0, b, 0)
    kern = functools.partial(_dense_attn_kernel, dv=dv, tq=tq, heads_per_iter=ATTN_HEADS_PER_ITER)
    return pl.pallas_call(
        kern,
        grid=(t // seq,),
        in_specs=[pl.BlockSpec((heads, seq, dq), blk),
                  pl.BlockSpec((kv_heads, seq, dq), blk),
                  pl.BlockSpec((kv_heads, seq, 2 * dv), blk),
                  pl.BlockSpec((heads, seq, dv), blk)],
        out_specs=pl.BlockSpec((heads, seq, dv), blk),
        out_shape=jax.ShapeDtypeStruct((heads, t, dv), BF16),
        scratch_shapes=[pltpu.VMEM((tq, seq), F32), pltpu.VMEM((tq, seq), F32)],
        compiler_params=pltpu.CompilerParams(dimension_semantics=("parallel",),
                                             vmem_limit_bytes=VMEM_LIMIT_BYTES),
        name=name,
    )(q, k, v_ext, gate)


def _nbr_attn_kernel(pid_ref, q_ref, qn_ref, k_ref, kn_ref, v_ref, bias_ref, g_ref,
                     o_ref, s_a, s_b, *, max_ws, nb):
    j = pl.program_id(0)
    i0 = (2 * j) % nb

    def window_start(i):
        ws = jnp.clip(C_QROWS * i - WIN_ROWS // 2, 0, max_ws)
        return pl.multiple_of(ws * GRID_W, GRID_W)

    def qk(q, keys_ref, i, s_ref, hd):
        sl = slice(hd * HEAD_DIM, (hd + 1) * HEAD_DIM)
        s_ref[hd] = lax.dot_general(q[:, sl], keys_ref[pl.ds(window_start(i), C_TK), sl], _NT,
                                    preferred_element_type=F32)

    def finish(s_ref, i, rows, hd):
        sl = slice(hd * HEAD_DIM, (hd + 1) * HEAD_DIM)
        vw = v_ref[pl.ds(window_start(i), C_TK), 2 * hd * HEAD_DIM:2 * (hd + 1) * HEAD_DIM]
        s = s_ref[hd] + bias_ref[pid_ref[i], hd]
        o = _softmax_pv(s, vw, HEAD_DIM) * g_ref[rows, sl].astype(F32)
        o_ref[rows, sl] = o.astype(BF16)

    first, second = slice(0, C_TQ), slice(C_TQ, 2 * C_TQ)

    @pl.when(j == 0)
    def _():
        for hd in range(C_HEADS):
            qk(q_ref[first, :], k_ref, i0, s_a, hd)

    for hd in range(C_HEADS):
        qk(q_ref[second, :], k_ref, i0 + 1, s_b, hd)
        finish(s_a, i0, first, hd)
    for hd in range(C_HEADS):
        qk(qn_ref[...], kn_ref, (i0 + 2) % nb, s_a, hd)
        finish(s_b, i0 + 1, second, hd)


def _nbr_attn(q, k, v, bias, pat_ids, layer, gate, seq):
    t = q.shape[0]
    nb = seq // C_TQ
    n_blocks = t // C_TQ
    assert nb % 2 == 0
    kern = functools.partial(_nbr_attn_kernel, max_ws=seq // GRID_W - C_WROWS, nb=nb)
    nxt = lambda j: jnp.minimum(2 * j + 2, n_blocks - 1)
    n_pat = bias.shape[0]
    bias_spec = pl.BlockSpec((n_pat, None, C_HEADS, C_TQ, C_TK),
                             lambda j, pid: (0, layer, 0, 0, 0), pipeline_mode=pl.Buffered(1))
    grid_spec = pltpu.PrefetchScalarGridSpec(
        num_scalar_prefetch=1,
        grid=(n_blocks // 2,),
        in_specs=[pl.BlockSpec((2 * C_TQ, C_W), lambda j, pid: (j, 0)),
                  pl.BlockSpec((C_TQ, C_W), lambda j, pid: (nxt(j), 0)),
                  pl.BlockSpec((seq, C_W), lambda j, pid: (2 * j // nb, 0)),
                  pl.BlockSpec((seq, C_W), lambda j, pid: (nxt(j) // nb, 0)),
                  pl.BlockSpec((seq, 2 * C_W), lambda j, pid: (2 * j // nb, 0)),
                  bias_spec,
                  pl.BlockSpec((2 * C_TQ, C_W), lambda j, pid: (j, 0))],
        out_specs=pl.BlockSpec((2 * C_TQ, C_W), lambda j, pid: (j, 0)),
        scratch_shapes=[pltpu.VMEM((C_HEADS, C_TQ, C_TK), F32)] * 2)
    return pl.pallas_call(
        kern,
        grid_spec=grid_spec,
        out_shape=jax.ShapeDtypeStruct((t, C_W), BF16),
        compiler_params=pltpu.CompilerParams(dimension_semantics=("arbitrary",),
                                             vmem_limit_bytes=VMEM_LIMIT_BYTES),
        name="attn_c",
    )(pat_ids, q, q, k, k, v, bias, gate)


def _nbr_bias(rpb, seq):
    depth, heads, n_dr, n_dc = rpb.shape
    g = depth * heads
    rows = seq // GRID_W
    nb = seq // C_TQ
    neg = NEG_BIG * LOG2E
    lead = GRID_W - WIN_COLS
    u = jnp.pad(rpb.reshape(g, n_dr, n_dc) * LOG2E,
                ((0, 0), (0, 1), (lead, 2 * GRID_W - n_dc - lead)))
    blk = np.arange(nb)[:, None, None]
    r = C_QROWS * blk + np.arange(C_QROWS)[None, :, None]
    ws = np.clip(C_QROWS * blk - WIN_ROWS // 2, 0, rows - C_WROWS)
    kr = ws + np.arange(C_WROWS)[None, None, :]
    r0 = np.clip(r - WIN_ROWS // 2, 0, rows - WIN_ROWS)
    valid_r = (kr >= r0) & (kr < r0 + WIN_ROWS)
    slab_id = np.where(valid_r, kr - r + (WIN_ROWS - 1), -1)
    pats, pat_ids = np.unique(slab_id.reshape(nb, -1), axis=0, return_inverse=True)
    pats = np.where(pats < 0, n_dr, pats).reshape(-1, C_QROWS, C_WROWS)
    n_pat = pats.shape[0]
    table = tuple(tuple(tuple(int(s) for s in row) for row in pat) for pat in pats)
    bias = pl.pallas_call(
        functools.partial(_nbr_bias_kernel, table=table, n_dr=n_dr, neg=neg),
        grid=(g,),
        in_specs=[pl.BlockSpec((None, n_dr + 1, 2 * GRID_W), lambda i: (i, 0, 0))],
        out_specs=pl.BlockSpec((n_pat, None, C_TQ, C_TK), lambda i: (0, i, 0, 0)),
        out_shape=jax.ShapeDtypeStruct((n_pat, g, C_TQ, C_TK), F32),
        scratch_shapes=[pltpu.VMEM((n_dr + 1, GRID_W, 2 * GRID_W), F32)],
        compiler_params=pltpu.CompilerParams(dimension_semantics=("parallel",)),
        name="nbr_bias",
    )(u)
    return (bias.reshape(n_pat, depth, heads, C_TQ, C_TK),
            jnp.asarray(pat_ids.reshape(nb), jnp.int32))


def _nbr_bias_kernel(u_ref, o_ref, slab_scr, *, table, n_dr, neg):
    shape = (GRID_W, 2 * GRID_W)
    lane = lax.broadcasted_iota(jnp.int32, shape, 1)
    qc = lax.broadcasted_iota(jnp.int32, shape, 0)
    kc = lane % GRID_W
    left = lane < GRID_W
    c0 = jnp.clip(qc - WIN_COLS // 2, 0, GRID_W - WIN_COLS)
    valid = (kc >= c0) & (kc < c0 + WIN_COLS)
    for dr in range(n_dr):
        rows = jnp.broadcast_to(u_ref[dr:dr + 1, :], shape)
        skew = pltpu.roll(rows, GRID_W + 1, 1, stride=1, stride_axis=0)
        both = jnp.where(left, skew, pltpu.roll(skew, GRID_W, 1))
        slab_scr[dr] = jnp.where(valid, both, neg)
    slab_scr[n_dr] = jnp.full(shape, neg, F32)
    for p, pat in enumerate(table):
        for q, row in enumerate(pat):
            for kp in range(len(row) // 2):
                tile = jnp.where(left, slab_scr[row[2 * kp]], slab_scr[row[2 * kp + 1]])
                o_ref[p, q * GRID_W:(q + 1) * GRID_W, 2 * kp * GRID_W:2 * (kp + 1) * GRID_W] = tile


def _outproj_kernel(ma_ref, mb_ref, mc_ref, x_ref, w_ref, gpost_ref, o_ref, y_a, y_b, *, n_tiles):
    i = pl.program_id(0)
    tm, d = o_ref.shape
    n_chunks = 4
    cols = d // n_chunks
    body_rows = (tm - SUBLANES) // (n_chunks - 1)
    row_starts = [c * body_rows for c in range(n_chunks)] + [tm]
    assert body_rows % SUBLANES == 0 and row_starts[-2] == tm - SUBLANES

    def dots(y_ref, c, after):
        heads_a = [ma_ref[hd] for hd in range(A_HEADS)]
        if after is not None:
            zero = pltpu.bitcast(after, F32)
            packed = 2 * SUBLANES
            top = heads_a[0][0:packed, :].astype(F32) + jnp.concatenate([zero, zero], axis=0)
            heads_a[0] = jnp.concatenate([top.astype(BF16), heads_a[0][packed:, :]], axis=0)
        ma = jnp.concatenate(heads_a, axis=1)
        mb = jnp.concatenate([mb_ref[hd] for hd in range(B_HEADS)], axis=1)
        cs = slice(c * cols, (c + 1) * cols)
        y = jnp.dot(ma, w_ref[0:A_Q, cs], preferred_element_type=F32)
        y += jnp.dot(mb, w_ref[A_Q:A_Q + B_G, cs], preferred_element_type=F32)
        y += jnp.dot(mc_ref[...], w_ref[A_Q + B_G:D_MIX, cs], preferred_element_type=F32)
        y_ref[:, cs] = y

    def epilogue(y_ref, c):
        rs = slice(row_starts[c], row_starts[c + 1])
        rows = row_starts[c + 1] - row_starts[c]
        out = x_ref[rs, :] + _rms(y_ref[rs, :], gpost_ref[...])
        o_ref[rs, :] = out
        bits = pltpu.bitcast(out, jnp.uint32)
        acc = None
        for r in range(0, rows, SUBLANES):
            for l in range(0, d, LANES):
                word = bits[r:r + SUBLANES, l:l + LANES]
                acc = word if acc is None else acc | word
        return (acc >> 16) >> 16

    def step(y_cur, y_prev):
        after = None
        for c in range(n_chunks):
            if y_cur is not None:
                dots(y_cur, c, after)
            if y_prev is not None:
                after = epilogue(y_prev, c)

    inner = (i > 0) & (i < n_tiles)

    @pl.when(i == 0)
    def _():
        step(y_a, None)

    @pl.when(inner & (i % 2 == 1))
    def _():
        step(y_b, y_a)

    @pl.when(inner & (i % 2 == 0))
    def _():
        step(y_a, y_b)

    @pl.when(i == n_tiles)
    def _():
        step(None, y_a if (n_tiles - 1) % 2 == 0 else y_b)


def _outproj(ma, mb, mc, x2d, w, gpost, tm):
    t, d = x2d.shape
    n_tiles = t // tm
    cur = lambda i: (jnp.minimum(i, n_tiles - 1), 0)
    cur_hm = lambda i: (0, jnp.minimum(i, n_tiles - 1), 0)
    prev = lambda i: (jnp.maximum(i - 1, 0), 0)
    fix = lambda i: (0, 0)
    return pl.pallas_call(
        functools.partial(_outproj_kernel, n_tiles=n_tiles),
        grid=(n_tiles + 1,),
        in_specs=[pl.BlockSpec((A_HEADS, tm, HEAD_DIM), cur_hm),
                  pl.BlockSpec((B_HEADS, tm, B_V), cur_hm),
                  pl.BlockSpec((tm, C_W), cur), pl.BlockSpec((tm, d), prev),
                  pl.BlockSpec(w.shape, fix, pipeline_mode=pl.Buffered(1)),
                  pl.BlockSpec(gpost.shape, fix, pipeline_mode=pl.Buffered(1))],
        out_specs=pl.BlockSpec((tm, d), prev),
        out_shape=jax.ShapeDtypeStruct((t, d), F32),
        scratch_shapes=[pltpu.VMEM((tm, d), F32), pltpu.VMEM((tm, d), F32)],
        compiler_params=pltpu.CompilerParams(dimension_semantics=("arbitrary",),
                                             vmem_limit_bytes=VMEM_LIMIT_BYTES),
        name="outproj",
    )(ma, mb, mc, x2d, w, gpost)


def _rope_tables(seq):
    t = np.arange(seq)
    pos = np.stack([t // GRID_W, t % GRID_W], axis=0).astype(np.float64)

    def tables(n, lanes_used):
        j = np.arange(LANES)
        axis = np.minimum(j // n, 1)
        i = j % (n // 2)
        inv_freq = 1.0 / (ROPE_THETA ** (np.arange(0, n, 2, dtype=np.float64) / n))
        ang = pos[axis, :].T * inv_freq[i][None, :]
        used = (j < lanes_used)[None, :]
        low = ((j % n) < n // 2)[None, :]
        cos = np.where(used, np.cos(ang), 0.0)
        sin = np.where(used, np.sin(ang), 0.0)
        return tuple(jnp.asarray(a, F32) for a in (cos, np.where(low, -sin, 0.0), np.where(low, 0.0, sin)))

    return tables(HEAD_DIM // 2, LANES) + tables(B_ROPE // 2, B_ROPE)


def _w_in_repack_kernel(w_ref, o_ref):
    kr_lo = OFF_BG
    kr_hi = kr_lo + B_ROPE
    tc = w_ref.shape[1]
    chunk = 256
    for lo in range(0, kr_lo, chunk):
        o_ref[:, lo:lo + chunk] = w_ref[lo:lo + chunk, :].T.astype(BF16)
    for lo in range(kr_lo, OFF_BKR, chunk):
        o_ref[:, lo:lo + chunk] = w_ref[lo + B_ROPE:lo + B_ROPE + chunk, :].T.astype(BF16)
    tail = jnp.concatenate([w_ref[kr_lo:kr_hi, :], jnp.zeros((LANES - B_ROPE, tc), F32)], axis=0)
    o_ref[:, OFF_BKR:W_IN_COLS] = tail.T.astype(BF16)


def _prep_w_in(w, tc=256):
    depth, d, n = w.shape
    wt = jnp.swapaxes(w, 1, 2)
    return pl.pallas_call(
        _w_in_repack_kernel,
        grid=(depth, d // tc),
        in_specs=[pl.BlockSpec((None, n, tc), lambda l, i: (l, 0, i))],
        out_specs=pl.BlockSpec((None, tc, W_IN_COLS), lambda l, i: (l, i, 0)),
        out_shape=jax.ShapeDtypeStruct((depth, d, W_IN_COLS), BF16),
        compiler_params=pltpu.CompilerParams(dimension_semantics=("parallel", "parallel"),
                                             vmem_limit_bytes=VMEM_LIMIT_BYTES),
        name="w_in_repack",
    )(wt)


def _prep_w_uq(w):
    r = w.shape[0]
    w = w.reshape(r, B_HEADS, B_NOPE + B_ROPE)
    w = jnp.pad(w, ((0, 0), (0, 0), (0, B_QK_PAD - B_NOPE - B_ROPE)))
    return w.reshape(r, B_HEADS * B_QK_PAD).astype(BF16)


def _prep_w_ukv(w):
    r = w.shape[0]
    w = w.reshape(r, B_HEADS, B_NOPE + B_V)
    wk = w[:, :, :B_NOPE].reshape(r, B_HEADS * B_NOPE)
    wv = w[:, :, B_NOPE:].reshape(r, B_HEADS * B_V)
    return wk.astype(BF16), wv.astype(BF16)


def kernel(x, norm_pre, norm_post, w_in, a_q_norm, a_k_norm, b_q_norm, b_kv_norm, b_w_uq, b_w_ukv, c_rpb, w_out):
    batch, seq, d = x.shape
    depth = w_in.shape[0]
    tokens = batch * seq
    assert seq % (2 * C_TQ) == 0 and seq // GRID_W >= C_WROWS
    assert w_in.shape[2] == W_IN_COLS - (LANES - B_ROPE)
    assert seq % INPROJ_ROWS == 0 and tokens % OUTPROJ_ROWS == 0 and seq % ATTN_QUERY_ROWS == 0
    tabs = _rope_tables(seq)
    bias, pat_ids = _nbr_bias(c_rpb, seq)
    w_in_all = _prep_w_in(w_in)
    h = x.reshape(tokens, d)
    for l in range(depth):
        wuk, wuv = _prep_w_ukv(b_w_ukv[l])
        qa, ka, va, ga, qb, kb, vb, gb, qc, kc, vc, gc = _inproj(
            h, seq, l, norm_pre[l][None], w_in_all, _prep_w_uq(b_w_uq[l]), wuk, wuv,
            a_q_norm[l][None], a_k_norm[l][None], b_q_norm[l][None], b_kv_norm[l][None],
            tabs, tm=INPROJ_ROWS)
        mix_a = _dense_attn(qa, ka, va, ga, seq, tq=ATTN_QUERY_ROWS, name="attn_a")
        mix_b = _dense_attn(qb, kb, vb, gb, seq, tq=ATTN_QUERY_ROWS, name="attn_b")
        mix_c = _nbr_attn(qc, kc, vc, bias, pat_ids, l, gc, seq)
        h = _outproj(mix_a, mix_b, mix_c, h, w_out[l].astype(BF16), norm_post[l][None],
                     tm=OUTPROJ_ROWS)
    return h.reshape(batch, seq, d)
```

```python
import functools

import jax
import jax.numpy as jnp
import numpy as np
from jax import lax
from jax.experimental import pallas as pl
from jax.experimental.pallas import tpu as pltpu

F32 = jnp.float32
BF16 = jnp.bfloat16
LANES = 128
SUBLANES = 8

GRID_W = 64
HEAD_DIM = 128
A_HEADS = 8
A_KV_HEADS = 2
B_HEADS = 4
B_Q_LORA = 512
B_KV_LORA = 256
B_NOPE = 128
B_ROPE = 64
B_V = 128
B_QK_PAD = 256
C_HEADS = 4
WIN_ROWS = 8
WIN_COLS = 16
ROPE_THETA = 10000.0
NORM_EPS = 1e-6
NEG_BIG = -1e30
LOG2E = 1.4426950408889634

A_Q = A_HEADS * HEAD_DIM
A_KV = A_KV_HEADS * HEAD_DIM
B_G = B_HEADS * B_V
C_W = C_HEADS * HEAD_DIM
D_MIX = A_Q + B_G + C_W

OFF_AQ = 0
OFF_AK = OFF_AQ + A_Q
OFF_AV = OFF_AK + A_KV
OFF_AG = OFF_AV + A_KV
OFF_BCQ = OFF_AG + A_Q
OFF_BCKV = OFF_BCQ + B_Q_LORA
OFF_BG = OFF_BCKV + B_KV_LORA
OFF_CQ = OFF_BG + B_G
OFF_CK = OFF_CQ + C_W
OFF_CV = OFF_CK + C_W
OFF_CG = OFF_CV + C_W
OFF_BKR = OFF_CG + C_W
W_IN_COLS = OFF_BKR + LANES

C_QROWS = 4
C_WROWS = 12
C_TQ = C_QROWS * GRID_W
C_TK = C_WROWS * GRID_W

VMEM_LIMIT_BYTES = 56 * 1024 * 1024

INPROJ_ROWS = 256
ATTN_QUERY_ROWS = 256
OUTPROJ_ROWS = 512
ATTN_HEADS_PER_ITER = 2


def _rms(x, gain):
    return x * lax.rsqrt(jnp.mean(x * x, axis=-1, keepdims=True) + NORM_EPS) * gain


def _rope(x, cos, sin_lo, sin_hi, half):
    up = pltpu.roll(x, LANES - half, 1)
    dn = pltpu.roll(x, half, 1)
    return x * cos + up * sin_lo + dn * sin_hi


def _silu(g):
    return g * (1.0 / (1.0 + jnp.exp(-g)))


_NT = (((1,), (1,)), ((), ()))


def _inproj_kernel(x_ref, gpre_ref, w_ref, wuq_ref, wuk_ref, wuv_ref,
                   aqn_ref, akn_ref, bqn_ref, bkvn_ref,
                   cos_a, slo_a, shi_a, cos_b, slo_b, shi_b,
                   qa_ref, ka_ref, va_ref, ga_ref, qb_ref, kb_ref, vb_ref, gb_ref,
                   qc_ref, kc_ref, vc_ref, gc_ref, h_scr):
    x = x_ref[...]
    h_scr[...] = (x * gpre_ref[...]).astype(BF16)
    inv_rms = lax.rsqrt(jnp.mean(x * x, axis=-1, keepdims=True) + NORM_EPS)
    tm = x.shape[0]

    def proj(off, width):
        return jnp.dot(h_scr[...], w_ref[:, off:off + width], preferred_element_type=F32) * inv_rms

    ones = jnp.ones((tm, HEAD_DIM), BF16)

    ca, sla, sha = cos_a[...], slo_a[...], shi_a[...]
    cb, slb, shb = cos_b[...], slo_b[...], shi_b[...]
    scale_a = HEAD_DIM ** -0.5 * LOG2E
    scale_b = (B_NOPE + B_ROPE) ** -0.5 * LOG2E

    cq = _rms(proj(OFF_BCQ, B_Q_LORA), bqn_ref[...]).astype(BF16)
    ckv = _rms(proj(OFF_BCKV, B_KV_LORA), bkvn_ref[...]).astype(BF16)
    kpe = _rope(proj(OFF_BKR, LANES), cb, slb, shb, 16).astype(BF16)

    pq = proj(OFF_AQ, A_Q)
    for hd in range(A_HEADS):
        sl = slice(hd * HEAD_DIM, (hd + 1) * HEAD_DIM)
        q = _rope(_rms(pq[:, sl], aqn_ref[...]), ca, sla, sha, 32)
        qa_ref[hd] = (q * scale_a).astype(BF16)
    pk = proj(OFF_AK, A_KV)
    pv = proj(OFF_AV, A_KV)
    for hd in range(A_KV_HEADS):
        sl = slice(hd * HEAD_DIM, (hd + 1) * HEAD_DIM)
        ka_ref[hd] = _rope(_rms(pk[:, sl], akn_ref[...]), ca, sla, sha, 32).astype(BF16)
        va_ref[hd, :, 0:HEAD_DIM] = pv[:, sl].astype(BF16)
        va_ref[hd, :, HEAD_DIM:2 * HEAD_DIM] = ones
    ga = _silu(proj(OFF_AG, A_Q)).astype(BF16)
    for hd in range(A_HEADS):
        ga_ref[hd] = ga[:, hd * HEAD_DIM:(hd + 1) * HEAD_DIM]

    qb = jnp.dot(cq, wuq_ref[...], preferred_element_type=F32)
    kn = jnp.dot(ckv, wuk_ref[...], preferred_element_type=F32)
    vb = jnp.dot(ckv, wuv_ref[...], preferred_element_type=F32)
    gb = _silu(proj(OFF_BG, B_G)).astype(BF16)
    for hd in range(B_HEADS):
        lo = hd * B_QK_PAD
        qb_ref[hd, :, 0:B_NOPE] = (qb[:, lo:lo + B_NOPE] * scale_b).astype(BF16)
        qpe = _rope(qb[:, lo + B_NOPE:lo + B_QK_PAD], cb, slb, shb, 16)
        qb_ref[hd, :, B_NOPE:B_QK_PAD] = (qpe * scale_b).astype(BF16)
        kb_ref[hd, :, 0:B_NOPE] = kn[:, hd * B_NOPE:(hd + 1) * B_NOPE].astype(BF16)
        kb_ref[hd, :, B_NOPE:B_QK_PAD] = kpe
        vb_ref[hd, :, 0:B_V] = vb[:, hd * B_V:(hd + 1) * B_V].astype(BF16)
        vb_ref[hd, :, B_V:2 * B_V] = ones
        gb_ref[hd] = gb[:, hd * B_V:(hd + 1) * B_V]

    gc_ref[...] = _silu(proj(OFF_CG, C_W)).astype(BF16)
    qc_ref[...] = (proj(OFF_CQ, C_W) * scale_a).astype(BF16)
    pvc = proj(OFF_CV, C_W)
    for hd in range(C_HEADS):
        vc_ref[:, 2 * hd * HEAD_DIM:(2 * hd + 1) * HEAD_DIM] = (
            pvc[:, hd * HEAD_DIM:(hd + 1) * HEAD_DIM].astype(BF16))
        vc_ref[:, (2 * hd + 1) * HEAD_DIM:(2 * hd + 2) * HEAD_DIM] = ones
    kc_ref[...] = proj(OFF_CK, C_W).astype(BF16)


def _layer_spec(a, layer):
    zeros = (0,) * (a.ndim - 1)
    return pl.BlockSpec((None,) + a.shape[1:], lambda i: (layer,) + zeros,
                        pipeline_mode=pl.Buffered(1))


def _inproj(x2d, seq, layer, gpre, w_all, wuq, wuk, wuv, aqn, akn, bqn, bkvn, tabs, tm):
    t, d = x2d.shape
    n_s = seq // tm
    row = lambda i: (i, 0)
    tab = lambda i: (i % n_s, 0)
    resident = functools.partial(_layer_spec, layer=layer)

    head_major = ((A_HEADS, HEAD_DIM), (A_KV_HEADS, HEAD_DIM), (A_KV_HEADS, 2 * HEAD_DIM),
                  (A_HEADS, HEAD_DIM), (B_HEADS, B_QK_PAD), (B_HEADS, B_QK_PAD),
                  (B_HEADS, 2 * B_V), (B_HEADS, B_V))
    token_major = (C_W, C_W, 2 * C_W, C_W)
    out_specs = ([pl.BlockSpec((nh, tm, n), lambda i: (0, i, 0)) for nh, n in head_major]
                 + [pl.BlockSpec((tm, n), row) for n in token_major])
    out_shape = ([jax.ShapeDtypeStruct((nh, t, n), BF16) for nh, n in head_major]
                 + [jax.ShapeDtypeStruct((t, n), BF16) for n in token_major])
    return pl.pallas_call(
        _inproj_kernel,
        grid=(t // tm,),
        in_specs=[pl.BlockSpec((tm, d), row), resident(gpre), resident(w_all), resident(wuq),
                  resident(wuk), resident(wuv), resident(aqn), resident(akn), resident(bqn),
                  resident(bkvn)] + [pl.BlockSpec((tm, LANES), tab)] * 6,
        out_specs=out_specs,
        out_shape=out_shape,
        scratch_shapes=[pltpu.VMEM((tm, d), BF16)],
        compiler_params=pltpu.CompilerParams(dimension_semantics=("parallel",),
                                             vmem_limit_bytes=VMEM_LIMIT_BYTES),
        name="inproj",
    )(x2d, gpre, w_all, wuq, wuk, wuv, aqn, akn, bqn, bkvn, *tabs)


def _softmax_pv(s, v_ext, dv):
    m = jnp.max(s, axis=-1, keepdims=True)
    p = jnp.exp2(s - m).astype(BF16)
    oe = jnp.dot(p, v_ext, preferred_element_type=F32)
    return oe[:, :dv] * (1.0 / oe[:, dv:])


def _dense_attn_kernel(q_ref, k_ref, v_ref, g_ref, o_ref, s_a, s_b, *, dv, tq, heads_per_iter):
    heads, seq, _ = q_ref.shape
    group = heads // k_ref.shape[0]
    n_t = seq // tq
    bufs = (s_a, s_b)

    def qk(hd, u, s_ref):
        s_ref[...] = lax.dot_general(q_ref[hd, u * tq:(u + 1) * tq, :], k_ref[hd // group], _NT,
                                     preferred_element_type=F32)

    def finish(hd, u, s_ref):
        o = _softmax_pv(s_ref[...], v_ref[hd // group], dv)
        rows = slice(u * tq, (u + 1) * tq)
        o_ref[hd, rows, :] = (o * g_ref[hd, rows, :].astype(F32)).astype(BF16)

    def body(it, carry):
        for h in range(heads_per_iter):
            hd = it * heads_per_iter + h
            nxt = jnp.minimum(hd + 1, heads - 1)
            for u in range(n_t):
                if u + 1 < n_t:
                    qk(hd, u + 1, bufs[(u + 1) % 2])
                else:
                    qk(nxt, 0, bufs[0])
                finish(hd, u, bufs[u % 2])
        return carry

    qk(0, 0, s_a)
    lax.fori_loop(0, heads // heads_per_iter, body, 0)


def _dense_attn(q, k, v_ext, gate, seq, *, tq, name):
    heads, t, dq = q.shape
    kv_heads = k.shape[0]
    dv = gate.shape[2]
    assert (seq // tq) % 2 == 0 and v_ext.shape[2] == 2 * dv
    assert heads % kv_heads == 0 and heads % ATTN_HEADS_PER_ITER == 0
    blk = lambda b: (0, b, 0)
    kern = functools.partial(_dense_attn_kernel, dv=dv, tq=tq, heads_per_iter=ATTN_HEADS_PER_ITER)
    return pl.pallas_call(
        kern,
        grid=(t // seq,),
        in_specs=[pl.BlockSpec((heads, seq, dq), blk),
                  pl.BlockSpec((kv_heads, seq, dq), blk),
                  pl.BlockSpec((kv_heads, seq, 2 * dv), blk),
                  pl.BlockSpec((heads, seq, dv), blk)],
        out_specs=pl.BlockSpec((heads, seq, dv), blk),
        out_shape=jax.ShapeDtypeStruct((heads, t, dv), BF16),
        scratch_shapes=[pltpu.VMEM((tq, seq), F32), pltpu.VMEM((tq, seq), F32)],
        compiler_params=pltpu.CompilerParams(dimension_semantics=("parallel",),
                                             vmem_limit_bytes=VMEM_LIMIT_BYTES),
        name=name,
    )(q, k, v_ext, gate)


def _nbr_attn_kernel(pid_ref, q_ref, qn_ref, k_ref, kn_ref, v_ref, bias_ref, g_ref,
                     o_ref, s_a, s_b, *, max_ws, nb):
    j = pl.program_id(0)
    i0 = (2 * j) % nb

    def window_start(i):
        ws = jnp.clip(C_QROWS * i - WIN_ROWS // 2, 0, max_ws)
        return pl.multiple_of(ws * GRID_W, GRID_W)

    def qk(q, keys_ref, i, s_ref, hd):
        sl = slice(hd * HEAD_DIM, (hd + 1) * HEAD_DIM)
        s_ref[hd] = lax.dot_general(q[:, sl], keys_ref[pl.ds(window_start(i), C_TK), sl], _NT,
                                    preferred_element_type=F32)

    def finish(s_ref, i, rows, hd):
        sl = slice(hd * HEAD_DIM, (hd + 1) * HEAD_DIM)
        vw = v_ref[pl.ds(window_start(i), C_TK), 2 * hd * HEAD_DIM:2 * (hd + 1) * HEAD_DIM]
        s = s_ref[hd] + bias_ref[pid_ref[i], hd]
        o = _softmax_pv(s, vw, HEAD_DIM) * g_ref[rows, sl].astype(F32)
        o_ref[rows, sl] = o.astype(BF16)

    first, second = slice(0, C_TQ), slice(C_TQ, 2 * C_TQ)

    @pl.when(j == 0)
    def _():
        for hd in range(C_HEADS):
            qk(q_ref[first, :], k_ref, i0, s_a, hd)

    for hd in range(C_HEADS):
        qk(q_ref[second, :], k_ref, i0 + 1, s_b, hd)
        finish(s_a, i0, first, hd)
    for hd in range(C_HEADS):
        qk(qn_ref[...], kn_ref, (i0 + 2) % nb, s_a, hd)
        finish(s_b, i0 + 1, second, hd)


def _nbr_attn(q, k, v, bias, pat_ids, layer, gate, seq):
    t = q.shape[0]
    nb = seq // C_TQ
    n_blocks = t // C_TQ
    assert nb % 2 == 0
    kern = functools.partial(_nbr_attn_kernel, max_ws=seq // GRID_W - C_WROWS, nb=nb)
    nxt = lambda j: jnp.minimum(2 * j + 2, n_blocks - 1)
    n_pat = bias.shape[0]
    bias_spec = pl.BlockSpec((n_pat, None, C_HEADS, C_TQ, C_TK),
                             lambda j, pid: (0, layer, 0, 0, 0), pipeline_mode=pl.Buffered(1))
    grid_spec = pltpu.PrefetchScalarGridSpec(
        num_scalar_prefetch=1,
        grid=(n_blocks // 2,),
        in_specs=[pl.BlockSpec((2 * C_TQ, C_W), lambda j, pid: (j, 0)),
                  pl.BlockSpec((C_TQ, C_W), lambda j, pid: (nxt(j), 0)),
                  pl.BlockSpec((seq, C_W), lambda j, pid: (2 * j // nb, 0)),
                  pl.BlockSpec((seq, C_W), lambda j, pid: (nxt(j) // nb, 0)),
                  pl.BlockSpec((seq, 2 * C_W), lambda j, pid: (2 * j // nb, 0)),
                  bias_spec,
                  pl.BlockSpec((2 * C_TQ, C_W), lambda j, pid: (j, 0))],
        out_specs=pl.BlockSpec((2 * C_TQ, C_W), lambda j, pid: (j, 0)),
        scratch_shapes=[pltpu.VMEM((C_HEADS, C_TQ, C_TK), F32)] * 2)
    return pl.pallas_call(
        kern,
        grid_spec=grid_spec,
        out_shape=jax.ShapeDtypeStruct((t, C_W), BF16),
        compiler_params=pltpu.CompilerParams(dimension_semantics=("arbitrary",),
                                             vmem_limit_bytes=VMEM_LIMIT_BYTES),
        name="attn_c",
    )(pat_ids, q, q, k, k, v, bias, gate)


def _nbr_bias(rpb, seq):
    depth, heads, n_dr, n_dc = rpb.shape
    g = depth * heads
    rows = seq // GRID_W
    nb = seq // C_TQ
    neg = NEG_BIG * LOG2E
    lead = GRID_W - WIN_COLS
    u = jnp.pad(rpb.reshape(g, n_dr, n_dc) * LOG2E,
                ((0, 0), (0, 1), (lead, 2 * GRID_W - n_dc - lead)))
    blk = np.arange(nb)[:, None, None]
    r = C_QROWS * blk + np.arange(C_QROWS)[None, :, None]
    ws = np.clip(C_QROWS * blk - WIN_ROWS // 2, 0, rows - C_WROWS)
    kr = ws + np.arange(C_WROWS)[None, None, :]
    r0 = np.clip(r - WIN_ROWS // 2, 0, rows - WIN_ROWS)
    valid_r = (kr >= r0) & (kr < r0 + WIN_ROWS)
    slab_id = np.where(valid_r, kr - r + (WIN_ROWS - 1), -1)
    pats, pat_ids = np.unique(slab_id.reshape(nb, -1), axis=0, return_inverse=True)
    pats = np.where(pats < 0, n_dr, pats).reshape(-1, C_QROWS, C_WROWS)
    n_pat = pats.shape[0]
    table = tuple(tuple(tuple(int(s) for s in row) for row in pat) for pat in pats)
    bias = pl.pallas_call(
        functools.partial(_nbr_bias_kernel, table=table, n_dr=n_dr, neg=neg),
        grid=(g,),
        in_specs=[pl.BlockSpec((None, n_dr + 1, 2 * GRID_W), lambda i: (i, 0, 0))],
        out_specs=pl.BlockSpec((n_pat, None, C_TQ, C_TK), lambda i: (0, i, 0, 0)),
        out_shape=jax.ShapeDtypeStruct((n_pat, g, C_TQ, C_TK), F32),
        scratch_shapes=[pltpu.VMEM((n_dr + 1, GRID_W, 2 * GRID_W), F32)],
        compiler_params=pltpu.CompilerParams(dimension_semantics=("parallel",)),
        name="nbr_bias",
    )(u)
    return (bias.reshape(n_pat, depth, heads, C_TQ, C_TK),
            jnp.asarray(pat_ids.reshape(nb), jnp.int32))


def _nbr_bias_kernel(u_ref, o_ref, slab_scr, *, table, n_dr, neg):
    shape = (GRID_W, 2 * GRID_W)
    lane = lax.broadcasted_iota(jnp.int32, shape, 1)
    qc = lax.broadcasted_iota(jnp.int32, shape, 0)
    kc = lane % GRID_W
    left = lane < GRID_W
    c0 = jnp.clip(qc - WIN_COLS // 2, 0, GRID_W - WIN_COLS)
    valid = (kc >= c0) & (kc < c0 + WIN_COLS)
    for dr in range(n_dr):
        rows = jnp.broadcast_to(u_ref[dr:dr + 1, :], shape)
        skew = pltpu.roll(rows, GRID_W + 1, 1, stride=1, stride_axis=0)
        both = jnp.where(left, skew, pltpu.roll(skew, GRID_W, 1))
        slab_scr[dr] = jnp.where(valid, both, neg)
    slab_scr[n_dr] = jnp.full(shape, neg, F32)
    for p, pat in enumerate(table):
        for q, row in enumerate(pat):
            for kp in range(len(row) // 2):
                tile = jnp.where(left, slab_scr[row[2 * kp]], slab_scr[row[2 * kp + 1]])
                o_ref[p, q * GRID_W:(q + 1) * GRID_W, 2 * kp * GRID_W:2 * (kp + 1) * GRID_W] = tile


def _outproj_kernel(ma_ref, mb_ref, mc_ref, x_ref, w_ref, gpost_ref, o_ref, y_a, y_b, *, n_tiles):
    i = pl.program_id(0)
    tm, d = o_ref.shape
    n_chunks = 4
    cols = d // n_chunks
    body_rows = (tm - SUBLANES) // (n_chunks - 1)
    row_starts = [c * body_rows for c in range(n_chunks)] + [tm]
    assert body_rows % SUBLANES == 0 and row_starts[-2] == tm - SUBLANES

    def dots(y_ref, c, after):
        heads_a = [ma_ref[hd] for hd in range(A_HEADS)]
        if after is not None:
            zero = pltpu.bitcast(after, F32)
            packed = 2 * SUBLANES
            top = heads_a[0][0:packed, :].astype(F32) + jnp.concatenate([zero, zero], axis=0)
            heads_a[0] = jnp.concatenate([top.astype(BF16), heads_a[0][packed:, :]], axis=0)
        ma = jnp.concatenate(heads_a, axis=1)
        mb = jnp.concatenate([mb_ref[hd] for hd in range(B_HEADS)], axis=1)
        cs = slice(c * cols, (c + 1) * cols)
        y = jnp.dot(ma, w_ref[0:A_Q, cs], preferred_element_type=F32)
        y += jnp.dot(mb, w_ref[A_Q:A_Q + B_G, cs], preferred_element_type=F32)
        y += jnp.dot(mc_ref[...], w_ref[A_Q + B_G:D_MIX, cs], preferred_element_type=F32)
        y_ref[:, cs] = y

    def epilogue(y_ref, c):
        rs = slice(row_starts[c], row_starts[c + 1])
        rows = row_starts[c + 1] - row_starts[c]
        out = x_ref[rs, :] + _rms(y_ref[rs, :], gpost_ref[...])
        o_ref[rs, :] = out
        bits = pltpu.bitcast(out, jnp.uint32)
        acc = None
        for r in range(0, rows, SUBLANES):
            for l in range(0, d, LANES):
                word = bits[r:r + SUBLANES, l:l + LANES]
                acc = word if acc is None else acc | word
        return (acc >> 16) >> 16

    def step(y_cur, y_prev):
        after = None
        for c in range(n_chunks):
            if y_cur is not None:
                dots(y_cur, c, after)
            if y_prev is not None:
                after = epilogue(y_prev, c)

    inner = (i > 0) & (i < n_tiles)

    @pl.when(i == 0)
    def _():
        step(y_a, None)

    @pl.when(inner & (i % 2 == 1))
    def _():
        step(y_b, y_a)

    @pl.when(inner & (i % 2 == 0))
    def _():
        step(y_a, y_b)

    @pl.when(i == n_tiles)
    def _():
        step(None, y_a if (n_tiles - 1) % 2 == 0 else y_b)


def _outproj(ma, mb, mc, x2d, layer, w_all, gpost_all, tm):
    t, d = x2d.shape
    n_tiles = t // tm
    cur = lambda i: (jnp.minimum(i, n_tiles - 1), 0)
    cur_hm = lambda i: (0, jnp.minimum(i, n_tiles - 1), 0)
    prev = lambda i: (jnp.maximum(i - 1, 0), 0)
    return pl.pallas_call(
        functools.partial(_outproj_kernel, n_tiles=n_tiles),
        grid=(n_tiles + 1,),
        in_specs=[pl.BlockSpec((A_HEADS, tm, HEAD_DIM), cur_hm),
                  pl.BlockSpec((B_HEADS, tm, B_V), cur_hm),
                  pl.BlockSpec((tm, C_W), cur), pl.BlockSpec((tm, d), prev),
                  _layer_spec(w_all, layer), _layer_spec(gpost_all, layer)],
        out_specs=pl.BlockSpec((tm, d), prev),
        out_shape=jax.ShapeDtypeStruct((t, d), F32),
        scratch_shapes=[pltpu.VMEM((tm, d), F32), pltpu.VMEM((tm, d), F32)],
        compiler_params=pltpu.CompilerParams(dimension_semantics=("arbitrary",),
                                             vmem_limit_bytes=VMEM_LIMIT_BYTES),
        name="outproj",
    )(ma, mb, mc, x2d, w_all, gpost_all)


def _rope_tables(seq):
    t = np.arange(seq)
    pos = np.stack([t // GRID_W, t % GRID_W], axis=0).astype(np.float64)

    def tables(n, lanes_used):
        j = np.arange(LANES)
        axis = np.minimum(j // n, 1)
        i = j % (n // 2)
        inv_freq = 1.0 / (ROPE_THETA ** (np.arange(0, n, 2, dtype=np.float64) / n))
        ang = pos[axis, :].T * inv_freq[i][None, :]
        used = (j < lanes_used)[None, :]
        low = ((j % n) < n // 2)[None, :]
        cos = np.where(used, np.cos(ang), 0.0)
        sin = np.where(used, np.sin(ang), 0.0)
        return tuple(jnp.asarray(a, F32) for a in (cos, np.where(low, -sin, 0.0), np.where(low, 0.0, sin)))

    return tables(HEAD_DIM // 2, LANES) + tables(B_ROPE // 2, B_ROPE)


def _w_in_repack_kernel(w_ref, o_ref):
    kr_lo = OFF_BG
    kr_hi = kr_lo + B_ROPE
    tc = w_ref.shape[1]
    chunk = 256
    for lo in range(0, kr_lo, chunk):
        o_ref[:, lo:lo + chunk] = w_ref[lo:lo + chunk, :].T.astype(BF16)
    for lo in range(kr_lo, OFF_BKR, chunk):
        o_ref[:, lo:lo + chunk] = w_ref[lo + B_ROPE:lo + B_ROPE + chunk, :].T.astype(BF16)
    tail = jnp.concatenate([w_ref[kr_lo:kr_hi, :], jnp.zeros((LANES - B_ROPE, tc), F32)], axis=0)
    o_ref[:, OFF_BKR:W_IN_COLS] = tail.T.astype(BF16)


def _prep_w_in(w, tc=256):
    depth, d, n = w.shape
    wt = jnp.swapaxes(w, 1, 2)
    return pl.pallas_call(
        _w_in_repack_kernel,
        grid=(depth, d // tc),
        in_specs=[pl.BlockSpec((None, n, tc), lambda l, i: (l, 0, i))],
        out_specs=pl.BlockSpec((None, tc, W_IN_COLS), lambda l, i: (l, i, 0)),
        out_shape=jax.ShapeDtypeStruct((depth, d, W_IN_COLS), BF16),
        compiler_params=pltpu.CompilerParams(dimension_semantics=("parallel", "parallel"),
                                             vmem_limit_bytes=VMEM_LIMIT_BYTES),
        name="w_in_repack",
    )(wt)


def _prep_w_uq(w):
    depth, r, _ = w.shape
    w = w.reshape(depth, r, B_HEADS, B_NOPE + B_ROPE)
    w = jnp.pad(w, ((0, 0), (0, 0), (0, 0), (0, B_QK_PAD - B_NOPE - B_ROPE)))
    return w.reshape(depth, r, B_HEADS * B_QK_PAD).astype(BF16)


def _prep_w_ukv(w):
    depth, r, _ = w.shape
    w = w.reshape(depth, r, B_HEADS, B_NOPE + B_V)
    wk = w[..., :B_NOPE].reshape(depth, r, B_HEADS * B_NOPE)
    wv = w[..., B_NOPE:].reshape(depth, r, B_HEADS * B_V)
    return wk.astype(BF16), wv.astype(BF16)


def kernel(x, norm_pre, norm_post, w_in, a_q_norm, a_k_norm, b_q_norm, b_kv_norm, b_w_uq, b_w_ukv, c_rpb, w_out):
    batch, seq, d = x.shape
    depth = w_in.shape[0]
    tokens = batch * seq
    assert seq % (2 * C_TQ) == 0 and seq // GRID_W >= C_WROWS
    assert w_in.shape[2] == W_IN_COLS - (LANES - B_ROPE)
    assert seq % INPROJ_ROWS == 0 and tokens % OUTPROJ_ROWS == 0 and seq % ATTN_QUERY_ROWS == 0
    tabs = _rope_tables(seq)
    bias, pat_ids = _nbr_bias(c_rpb, seq)
    w_in_all = _prep_w_in(w_in)
    w_out_all = w_out.astype(BF16)
    wuq, (wuk, wuv) = _prep_w_uq(b_w_uq), _prep_w_ukv(b_w_ukv)
    gpre, gpost, aqn, akn, bqn, bkvn = (
        g[:, None, :] for g in (norm_pre, norm_post, a_q_norm, a_k_norm, b_q_norm, b_kv_norm))
    h = x.reshape(tokens, d)
    for l in range(depth):
        qa, ka, va, ga, qb, kb, vb, gb, qc, kc, vc, gc = _inproj(
            h, seq, l, gpre, w_in_all, wuq, wuk, wuv, aqn, akn, bqn, bkvn, tabs, tm=INPROJ_ROWS)
        mix_a = _dense_attn(qa, ka, va, ga, seq, tq=ATTN_QUERY_ROWS, name="attn_a")
        mix_b = _dense_attn(qb, kb, vb, gb, seq, tq=ATTN_QUERY_ROWS, name="attn_b")
        mix_c = _nbr_attn(qc, kc, vc, bias, pat_ids, l, gc, seq)
        h = _outproj(mix_a, mix_b, mix_c, h, l, w_out_all, gpost, tm=OUTPROJ_ROWS)
    return h.reshape(batch, seq, d)
```

```python
import functools

import jax
import jax.numpy as jnp
import numpy as np
from jax import lax
from jax.experimental import pallas as pl
from jax.experimental.pallas import tpu as pltpu

F32 = jnp.float32
BF16 = jnp.bfloat16
LANES = 128
SUBLANES = 8

GRID_W = 64
HEAD_DIM = 128
A_HEADS = 8
A_KV_HEADS = 2
B_HEADS = 4
B_Q_LORA = 512
B_KV_LORA = 256
B_NOPE = 128
B_ROPE = 64
B_V = 128
B_QK_PAD = 256
C_HEADS = 4
WIN_ROWS = 8
WIN_COLS = 16
ROPE_THETA = 10000.0
NORM_EPS = 1e-6
NEG_BIG = -1e30
LOG2E = 1.4426950408889634

A_Q = A_HEADS * HEAD_DIM
A_KV = A_KV_HEADS * HEAD_DIM
B_G = B_HEADS * B_V
C_W = C_HEADS * HEAD_DIM
D_MIX = A_Q + B_G + C_W

OFF_AQ = 0
OFF_AK = OFF_AQ + A_Q
OFF_AV = OFF_AK + A_KV
OFF_AG = OFF_AV + A_KV
OFF_BCQ = OFF_AG + A_Q
OFF_BCKV = OFF_BCQ + B_Q_LORA
OFF_BG = OFF_BCKV + B_KV_LORA
OFF_CQ = OFF_BG + B_G
OFF_CK = OFF_CQ + C_W
OFF_CV = OFF_CK + C_W
OFF_CG = OFF_CV + C_W
OFF_BKR = OFF_CG + C_W
W_IN_COLS = OFF_BKR + LANES

C_QROWS = 4
C_WROWS = 12
C_TQ = C_QROWS * GRID_W
C_TK = C_WROWS * GRID_W

VMEM_LIMIT_BYTES = 56 * 1024 * 1024

INPROJ_ROWS = 256
ATTN_QUERY_ROWS = 256
OUTPROJ_ROWS = 512
ATTN_HEADS_PER_ITER = 2


def _rms(x, gain):
    return x * lax.rsqrt(jnp.mean(x * x, axis=-1, keepdims=True) + NORM_EPS) * gain


def _rope(x, cos, sin_lo, sin_hi, half):
    up = pltpu.roll(x, LANES - half, 1)
    dn = pltpu.roll(x, half, 1)
    return x * cos + up * sin_lo + dn * sin_hi


def _silu(g):
    return g * (1.0 / (1.0 + jnp.exp(-g)))


_NT = (((1,), (1,)), ((), ()))


W_IN_SEGMENTS = ((OFF_BCQ, B_Q_LORA), (OFF_BCKV, B_KV_LORA), (OFF_BKR, LANES), (OFF_AQ, A_Q),
                 (OFF_AK, A_KV), (OFF_AV, A_KV), (OFF_AG, A_Q), (OFF_BG, B_G), (OFF_CG, C_W),
                 (OFF_CQ, C_W), (OFF_CV, C_W), (OFF_CK, C_W))


def _inproj_kernel(x_ref, gpre_ref, w_hbm, *rest, layer):
    *refs, w_vmem, w_sem = rest
    first = pl.program_id(0) == 0

    def copy(seg):
        off, width = W_IN_SEGMENTS[seg]
        return pltpu.make_async_copy(w_hbm.at[layer, :, pl.ds(off, width)],
                                     w_vmem.at[:, pl.ds(off, width)], w_sem.at[seg])

    @pl.when(first)
    def _():
        for seg in range(len(W_IN_SEGMENTS)):
            copy(seg).start()
        _inproj_body(x_ref, gpre_ref, w_vmem, *refs,
                     wait=lambda off: copy([o for o, _ in W_IN_SEGMENTS].index(off)).wait())

    @pl.when(jnp.logical_not(first))
    def _():
        _inproj_body(x_ref, gpre_ref, w_vmem, *refs, wait=lambda off: None)


def _inproj_body(x_ref, gpre_ref, w_ref, wuq_ref, wuk_ref, wuv_ref,
                 aqn_ref, akn_ref, bqn_ref, bkvn_ref,
                 cos_a, slo_a, shi_a, cos_b, slo_b, shi_b,
                 qa_ref, ka_ref, va_ref, ga_ref, qb_ref, kb_ref, vb_ref, gb_ref,
                 qc_ref, kc_ref, vc_ref, gc_ref, h_scr, *, wait):
    x = x_ref[...]
    h_scr[...] = (x * gpre_ref[...]).astype(BF16)
    inv_rms = lax.rsqrt(jnp.mean(x * x, axis=-1, keepdims=True) + NORM_EPS)
    tm = x.shape[0]

    def proj(off, width):
        wait(off)
        return jnp.dot(h_scr[...], w_ref[:, off:off + width], preferred_element_type=F32) * inv_rms

    ones = jnp.ones((tm, HEAD_DIM), BF16)

    ca, sla, sha = cos_a[...], slo_a[...], shi_a[...]
    cb, slb, shb = cos_b[...], slo_b[...], shi_b[...]
    scale_a = HEAD_DIM ** -0.5 * LOG2E
    scale_b = (B_NOPE + B_ROPE) ** -0.5 * LOG2E

    cq = _rms(proj(OFF_BCQ, B_Q_LORA), bqn_ref[...]).astype(BF16)
    ckv = _rms(proj(OFF_BCKV, B_KV_LORA), bkvn_ref[...]).astype(BF16)
    kpe = _rope(proj(OFF_BKR, LANES), cb, slb, shb, 16).astype(BF16)

    pq = proj(OFF_AQ, A_Q)
    for hd in range(A_HEADS):
        sl = slice(hd * HEAD_DIM, (hd + 1) * HEAD_DIM)
        q = _rope(_rms(pq[:, sl], aqn_ref[...]), ca, sla, sha, 32)
        qa_ref[hd] = (q * scale_a).astype(BF16)
    pk = proj(OFF_AK, A_KV)
    pv = proj(OFF_AV, A_KV)
    for hd in range(A_KV_HEADS):
        sl = slice(hd * HEAD_DIM, (hd + 1) * HEAD_DIM)
        ka_ref[hd] = _rope(_rms(pk[:, sl], akn_ref[...]), ca, sla, sha, 32).astype(BF16)
        va_ref[hd, :, 0:HEAD_DIM] = pv[:, sl].astype(BF16)
        va_ref[hd, :, HEAD_DIM:2 * HEAD_DIM] = ones
    ga = _silu(proj(OFF_AG, A_Q)).astype(BF16)
    for hd in range(A_HEADS):
        ga_ref[hd] = ga[:, hd * HEAD_DIM:(hd + 1) * HEAD_DIM]

    qb = jnp.dot(cq, wuq_ref[...], preferred_element_type=F32)
    kn = jnp.dot(ckv, wuk_ref[...], preferred_element_type=F32)
    vb = jnp.dot(ckv, wuv_ref[...], preferred_element_type=F32)
    gb = _silu(proj(OFF_BG, B_G)).astype(BF16)
    for hd in range(B_HEADS):
        lo = hd * B_QK_PAD
        qb_ref[hd, :, 0:B_NOPE] = (qb[:, lo:lo + B_NOPE] * scale_b).astype(BF16)
        qpe = _rope(qb[:, lo + B_NOPE:lo + B_QK_PAD], cb, slb, shb, 16)
        qb_ref[hd, :, B_NOPE:B_QK_PAD] = (qpe * scale_b).astype(BF16)
        kb_ref[hd, :, 0:B_NOPE] = kn[:, hd * B_NOPE:(hd + 1) * B_NOPE].astype(BF16)
        kb_ref[hd, :, B_NOPE:B_QK_PAD] = kpe
        vb_ref[hd, :, 0:B_V] = vb[:, hd * B_V:(hd + 1) * B_V].astype(BF16)
        vb_ref[hd, :, B_V:2 * B_V] = ones
        gb_ref[hd] = gb[:, hd * B_V:(hd + 1) * B_V]

    gc_ref[...] = _silu(proj(OFF_CG, C_W)).astype(BF16)
    qc_ref[...] = (proj(OFF_CQ, C_W) * scale_a).astype(BF16)
    pvc = proj(OFF_CV, C_W)
    for hd in range(C_HEADS):
        vc_ref[:, 2 * hd * HEAD_DIM:(2 * hd + 1) * HEAD_DIM] = (
            pvc[:, hd * HEAD_DIM:(hd + 1) * HEAD_DIM].astype(BF16))
        vc_ref[:, (2 * hd + 1) * HEAD_DIM:(2 * hd + 2) * HEAD_DIM] = ones
    kc_ref[...] = proj(OFF_CK, C_W).astype(BF16)


def _layer_spec(a, layer):
    zeros = (0,) * (a.ndim - 1)
    return pl.BlockSpec((None,) + a.shape[1:], lambda i: (layer,) + zeros,
                        pipeline_mode=pl.Buffered(1))


def _inproj(x2d, seq, layer, gpre, w_all, wuq, wuk, wuv, aqn, akn, bqn, bkvn, tabs, tm):
    t, d = x2d.shape
    n_s = seq // tm
    row = lambda i: (i, 0)
    tab = lambda i: (i % n_s, 0)
    resident = functools.partial(_layer_spec, layer=layer)

    head_major = ((A_HEADS, HEAD_DIM), (A_KV_HEADS, HEAD_DIM), (A_KV_HEADS, 2 * HEAD_DIM),
                  (A_HEADS, HEAD_DIM), (B_HEADS, B_QK_PAD), (B_HEADS, B_QK_PAD),
                  (B_HEADS, 2 * B_V), (B_HEADS, B_V))
    token_major = (C_W, C_W, 2 * C_W, C_W)
    out_specs = ([pl.BlockSpec((nh, tm, n), lambda i: (0, i, 0)) for nh, n in head_major]
                 + [pl.BlockSpec((tm, n), row) for n in token_major])
    out_shape = ([jax.ShapeDtypeStruct((nh, t, n), BF16) for nh, n in head_major]
                 + [jax.ShapeDtypeStruct((t, n), BF16) for n in token_major])
    return pl.pallas_call(
        functools.partial(_inproj_kernel, layer=layer),
        grid=(t // tm,),
        in_specs=[pl.BlockSpec((tm, d), row), resident(gpre), pl.BlockSpec(memory_space=pl.ANY),
                  resident(wuq), resident(wuk), resident(wuv), resident(aqn), resident(akn),
                  resident(bqn), resident(bkvn)] + [pl.BlockSpec((tm, LANES), tab)] * 6,
        out_specs=out_specs,
        out_shape=out_shape,
        scratch_shapes=[pltpu.VMEM((tm, d), BF16), pltpu.VMEM(w_all.shape[1:], BF16),
                        pltpu.SemaphoreType.DMA((len(W_IN_SEGMENTS),))],
        compiler_params=pltpu.CompilerParams(dimension_semantics=("arbitrary",),
                                             vmem_limit_bytes=VMEM_LIMIT_BYTES),
        name="inproj",
    )(x2d, gpre, w_all, wuq, wuk, wuv, aqn, akn, bqn, bkvn, *tabs)


def _softmax_pv(s, v_ext, dv):
    m = jnp.max(s, axis=-1, keepdims=True)
    p = jnp.exp2(s - m).astype(BF16)
    oe = jnp.dot(p, v_ext, preferred_element_type=F32)
    return oe[:, :dv] * (1.0 / oe[:, dv:])


def _dense_attn_kernel(q_ref, k_ref, v_ref, g_ref, o_ref, s_a, s_b, *, dv, tq, heads_per_iter):
    heads, seq, _ = q_ref.shape
    group = heads // k_ref.shape[0]
    n_t = seq // tq
    bufs = (s_a, s_b)

    def qk(hd, u, s_ref):
        s_ref[...] = lax.dot_general(q_ref[hd, u * tq:(u + 1) * tq, :], k_ref[hd // group], _NT,
                                     preferred_element_type=F32)

    def finish(hd, u, s_ref):
        o = _softmax_pv(s_ref[...], v_ref[hd // group], dv)
        rows = slice(u * tq, (u + 1) * tq)
        o_ref[hd, rows, :] = (o * g_ref[hd, rows, :].astype(F32)).astype(BF16)

    def body(it, carry):
        for h in range(heads_per_iter):
            hd = it * heads_per_iter + h
            nxt = jnp.minimum(hd + 1, heads - 1)
            for u in range(n_t):
                if u + 1 < n_t:
                    qk(hd, u + 1, bufs[(u + 1) % 2])
                else:
                    qk(nxt, 0, bufs[0])
                finish(hd, u, bufs[u % 2])
        return carry

    qk(0, 0, s_a)
    lax.fori_loop(0, heads // heads_per_iter, body, 0)


def _dense_attn(q, k, v_ext, gate, seq, *, tq, name):
    heads, t, dq = q.shape
    kv_heads = k.shape[0]
    dv = gate.shape[2]
    assert (seq // tq) % 2 == 0 and v_ext.shape[2] == 2 * dv
    assert heads % kv_heads == 0 and heads % ATTN_HEADS_PER_ITER == 0
    blk = lambda b: (0, b, 0)
    kern = functools.partial(_dense_attn_kernel, dv=dv, tq=tq, heads_per_iter=ATTN_HEADS_PER_ITER)
    return pl.pallas_call(
        kern,
        grid=(t // seq,),
        in_specs=[pl.BlockSpec((heads, seq, dq), blk),
                  pl.BlockSpec((kv_heads, seq, dq), blk),
                  pl.BlockSpec((kv_heads, seq, 2 * dv), blk),
                  pl.BlockSpec((heads, seq, dv), blk)],
        out_specs=pl.BlockSpec((heads, seq, dv), blk),
        out_shape=jax.ShapeDtypeStruct((heads, t, dv), BF16),
        scratch_shapes=[pltpu.VMEM((tq, seq), F32), pltpu.VMEM((tq, seq), F32)],
        compiler_params=pltpu.CompilerParams(dimension_semantics=("parallel",),
                                             vmem_limit_bytes=VMEM_LIMIT_BYTES),
        name=name,
    )(q, k, v_ext, gate)


def _nbr_attn_kernel(pid_ref, q_ref, qn_ref, k_ref, kn_ref, v_ref, bias_ref, g_ref,
                     o_ref, s_a, s_b, *, max_ws, nb):
    j = pl.program_id(0)
    i0 = (2 * j) % nb

    def window_start(i):
        ws = jnp.clip(C_QROWS * i - WIN_ROWS // 2, 0, max_ws)
        return pl.multiple_of(ws * GRID_W, GRID_W)

    def qk(q, keys_ref, i, s_ref, hd):
        sl = slice(hd * HEAD_DIM, (hd + 1) * HEAD_DIM)
        s_ref[hd] = lax.dot_general(q[:, sl], keys_ref[pl.ds(window_start(i), C_TK), sl], _NT,
                                    preferred_element_type=F32)

    def finish(s_ref, i, rows, hd):
        sl = slice(hd * HEAD_DIM, (hd + 1) * HEAD_DIM)
        vw = v_ref[pl.ds(window_start(i), C_TK), 2 * hd * HEAD_DIM:2 * (hd + 1) * HEAD_DIM]
        s = s_ref[hd] + bias_ref[pid_ref[i], hd]
        o = _softmax_pv(s, vw, HEAD_DIM) * g_ref[rows, sl].astype(F32)
        o_ref[rows, sl] = o.astype(BF16)

    first, second = slice(0, C_TQ), slice(C_TQ, 2 * C_TQ)

    @pl.when(j == 0)
    def _():
        for hd in range(C_HEADS):
            qk(q_ref[first, :], k_ref, i0, s_a, hd)

    for hd in range(C_HEADS):
        qk(q_ref[second, :], k_ref, i0 + 1, s_b, hd)
        finish(s_a, i0, first, hd)
    for hd in range(C_HEADS):
        qk(qn_ref[...], kn_ref, (i0 + 2) % nb, s_a, hd)
        finish(s_b, i0 + 1, second, hd)


def _nbr_attn(q, k, v, bias, pat_ids, layer, gate, seq):
    t = q.shape[0]
    nb = seq // C_TQ
    n_blocks = t // C_TQ
    assert nb % 2 == 0
    kern = functools.partial(_nbr_attn_kernel, max_ws=seq // GRID_W - C_WROWS, nb=nb)
    nxt = lambda j: jnp.minimum(2 * j + 2, n_blocks - 1)
    n_pat = bias.shape[0]
    bias_spec = pl.BlockSpec((n_pat, None, C_HEADS, C_TQ, C_TK),
                             lambda j, pid: (0, layer, 0, 0, 0), pipeline_mode=pl.Buffered(1))
    grid_spec = pltpu.PrefetchScalarGridSpec(
        num_scalar_prefetch=1,
        grid=(n_blocks // 2,),
        in_specs=[pl.BlockSpec((2 * C_TQ, C_W), lambda j, pid: (j, 0)),
                  pl.BlockSpec((C_TQ, C_W), lambda j, pid: (nxt(j), 0)),
                  pl.BlockSpec((seq, C_W), lambda j, pid: (2 * j // nb, 0)),
                  pl.BlockSpec((seq, C_W), lambda j, pid: (nxt(j) // nb, 0)),
                  pl.BlockSpec((seq, 2 * C_W), lambda j, pid: (2 * j // nb, 0)),
                  bias_spec,
                  pl.BlockSpec((2 * C_TQ, C_W), lambda j, pid: (j, 0))],
        out_specs=pl.BlockSpec((2 * C_TQ, C_W), lambda j, pid: (j, 0)),
        scratch_shapes=[pltpu.VMEM((C_HEADS, C_TQ, C_TK), F32)] * 2)
    return pl.pallas_call(
        kern,
        grid_spec=grid_spec,
        out_shape=jax.ShapeDtypeStruct((t, C_W), BF16),
        compiler_params=pltpu.CompilerParams(dimension_semantics=("arbitrary",),
                                             vmem_limit_bytes=VMEM_LIMIT_BYTES),
        name="attn_c",
    )(pat_ids, q, q, k, k, v, bias, gate)


def _nbr_bias(rpb, seq):
    depth, heads, n_dr, n_dc = rpb.shape
    g = depth * heads
    rows = seq // GRID_W
    nb = seq // C_TQ
    neg = NEG_BIG * LOG2E
    lead = GRID_W - WIN_COLS
    u = jnp.pad(rpb.reshape(g, n_dr, n_dc) * LOG2E,
                ((0, 0), (0, 1), (lead, 2 * GRID_W - n_dc - lead)))
    blk = np.arange(nb)[:, None, None]
    r = C_QROWS * blk + np.arange(C_QROWS)[None, :, None]
    ws = np.clip(C_QROWS * blk - WIN_ROWS // 2, 0, rows - C_WROWS)
    kr = ws + np.arange(C_WROWS)[None, None, :]
    r0 = np.clip(r - WIN_ROWS // 2, 0, rows - WIN_ROWS)
    valid_r = (kr >= r0) & (kr < r0 + WIN_ROWS)
    slab_id = np.where(valid_r, kr - r + (WIN_ROWS - 1), -1)
    pats, pat_ids = np.unique(slab_id.reshape(nb, -1), axis=0, return_inverse=True)
    pats = np.where(pats < 0, n_dr, pats).reshape(-1, C_QROWS, C_WROWS)
    n_pat = pats.shape[0]
    table = tuple(tuple(tuple(int(s) for s in row) for row in pat) for pat in pats)
    bias = pl.pallas_call(
        functools.partial(_nbr_bias_kernel, table=table, n_dr=n_dr, neg=neg),
        grid=(g,),
        in_specs=[pl.BlockSpec((None, n_dr + 1, 2 * GRID_W), lambda i: (i, 0, 0))],
        out_specs=pl.BlockSpec((n_pat, None, C_TQ, C_TK), lambda i: (0, i, 0, 0)),
        out_shape=jax.ShapeDtypeStruct((n_pat, g, C_TQ, C_TK), F32),
        scratch_shapes=[pltpu.VMEM((n_dr + 1, GRID_W, 2 * GRID_W), F32)],
        compiler_params=pltpu.CompilerParams(dimension_semantics=("parallel",)),
        name="nbr_bias",
    )(u)
    return (bias.reshape(n_pat, depth, heads, C_TQ, C_TK),
            jnp.asarray(pat_ids.reshape(nb), jnp.int32))


def _nbr_bias_kernel(u_ref, o_ref, slab_scr, *, table, n_dr, neg):
    shape = (GRID_W, 2 * GRID_W)
    lane = lax.broadcasted_iota(jnp.int32, shape, 1)
    qc = lax.broadcasted_iota(jnp.int32, shape, 0)
    kc = lane % GRID_W
    left = lane < GRID_W
    c0 = jnp.clip(qc - WIN_COLS // 2, 0, GRID_W - WIN_COLS)
    valid = (kc >= c0) & (kc < c0 + WIN_COLS)
    for dr in range(n_dr):
        rows = jnp.broadcast_to(u_ref[dr:dr + 1, :], shape)
        skew = pltpu.roll(rows, GRID_W + 1, 1, stride=1, stride_axis=0)
        both = jnp.where(left, skew, pltpu.roll(skew, GRID_W, 1))
        slab_scr[dr] = jnp.where(valid, both, neg)
    slab_scr[n_dr] = jnp.full(shape, neg, F32)
    for p, pat in enumerate(table):
        for q, row in enumerate(pat):
            for kp in range(len(row) // 2):
                tile = jnp.where(left, slab_scr[row[2 * kp]], slab_scr[row[2 * kp + 1]])
                o_ref[p, q * GRID_W:(q + 1) * GRID_W, 2 * kp * GRID_W:2 * (kp + 1) * GRID_W] = tile


def _outproj_kernel(ma_ref, mb_ref, mc_ref, x_ref, w_ref, gpost_ref, o_ref, y_a, y_b, *, n_tiles):
    i = pl.program_id(0)
    tm, d = o_ref.shape
    n_chunks = 4
    cols = d // n_chunks
    body_rows = (tm - SUBLANES) // (n_chunks - 1)
    row_starts = [c * body_rows for c in range(n_chunks)] + [tm]
    assert body_rows % SUBLANES == 0 and row_starts[-2] == tm - SUBLANES

    def dots(y_ref, c, after):
        heads_a = [ma_ref[hd] for hd in range(A_HEADS)]
        if after is not None:
            zero = pltpu.bitcast(after, F32)
            packed = 2 * SUBLANES
            top = heads_a[0][0:packed, :].astype(F32) + jnp.concatenate([zero, zero], axis=0)
            heads_a[0] = jnp.concatenate([top.astype(BF16), heads_a[0][packed:, :]], axis=0)
        ma = jnp.concatenate(heads_a, axis=1)
        mb = jnp.concatenate([mb_ref[hd] for hd in range(B_HEADS)], axis=1)
        cs = slice(c * cols, (c + 1) * cols)
        y = jnp.dot(ma, w_ref[0:A_Q, cs], preferred_element_type=F32)
        y += jnp.dot(mb, w_ref[A_Q:A_Q + B_G, cs], preferred_element_type=F32)
        y += jnp.dot(mc_ref[...], w_ref[A_Q + B_G:D_MIX, cs], preferred_element_type=F32)
        y_ref[:, cs] = y

    def epilogue(y_ref, c):
        rs = slice(row_starts[c], row_starts[c + 1])
        rows = row_starts[c + 1] - row_starts[c]
        out = x_ref[rs, :] + _rms(y_ref[rs, :], gpost_ref[...])
        o_ref[rs, :] = out
        bits = pltpu.bitcast(out, jnp.uint32)
        acc = None
        for r in range(0, rows, SUBLANES):
            for l in range(0, d, LANES):
                word = bits[r:r + SUBLANES, l:l + LANES]
                acc = word if acc is None else acc | word
        return (acc >> 16) >> 16

    def step(y_cur, y_prev):
        after = None
        for c in range(n_chunks):
            if y_cur is not None:
                dots(y_cur, c, after)
            if y_prev is not None:
                after = epilogue(y_prev, c)

    inner = (i > 0) & (i < n_tiles)

    @pl.when(i == 0)
    def _():
        step(y_a, None)

    @pl.when(inner & (i % 2 == 1))
    def _():
        step(y_b, y_a)

    @pl.when(inner & (i % 2 == 0))
    def _():
        step(y_a, y_b)

    @pl.when(i == n_tiles)
    def _():
        step(None, y_a if (n_tiles - 1) % 2 == 0 else y_b)


def _outproj(ma, mb, mc, x2d, layer, w_all, gpost_all, tm):
    t, d = x2d.shape
    n_tiles = t // tm
    cur = lambda i: (jnp.minimum(i, n_tiles - 1), 0)
    cur_hm = lambda i: (0, jnp.minimum(i, n_tiles - 1), 0)
    prev = lambda i: (jnp.maximum(i - 1, 0), 0)
    return pl.pallas_call(
        functools.partial(_outproj_kernel, n_tiles=n_tiles),
        grid=(n_tiles + 1,),
        in_specs=[pl.BlockSpec((A_HEADS, tm, HEAD_DIM), cur_hm),
                  pl.BlockSpec((B_HEADS, tm, B_V), cur_hm),
                  pl.BlockSpec((tm, C_W), cur), pl.BlockSpec((tm, d), prev),
                  _layer_spec(w_all, layer), _layer_spec(gpost_all, layer)],
        out_specs=pl.BlockSpec((tm, d), prev),
        out_shape=jax.ShapeDtypeStruct((t, d), F32),
        scratch_shapes=[pltpu.VMEM((tm, d), F32), pltpu.VMEM((tm, d), F32)],
        compiler_params=pltpu.CompilerParams(dimension_semantics=("arbitrary",),
                                             vmem_limit_bytes=VMEM_LIMIT_BYTES),
        name="outproj",
    )(ma, mb, mc, x2d, w_all, gpost_all)


def _rope_tables(seq):
    t = np.arange(seq)
    pos = np.stack([t // GRID_W, t % GRID_W], axis=0).astype(np.float64)

    def tables(n, lanes_used):
        j = np.arange(LANES)
        axis = np.minimum(j // n, 1)
        i = j % (n // 2)
        inv_freq = 1.0 / (ROPE_THETA ** (np.arange(0, n, 2, dtype=np.float64) / n))
        ang = pos[axis, :].T * inv_freq[i][None, :]
        used = (j < lanes_used)[None, :]
        low = ((j % n) < n // 2)[None, :]
        cos = np.where(used, np.cos(ang), 0.0)
        sin = np.where(used, np.sin(ang), 0.0)
        return tuple(jnp.asarray(a, F32) for a in (cos, np.where(low, -sin, 0.0), np.where(low, 0.0, sin)))

    return tables(HEAD_DIM // 2, LANES) + tables(B_ROPE // 2, B_ROPE)


def _w_in_repack_kernel(w_ref, o_ref):
    kr_lo = OFF_BG
    kr_hi = kr_lo + B_ROPE
    tc = w_ref.shape[1]
    chunk = 256
    for lo in range(0, kr_lo, chunk):
        o_ref[:, lo:lo + chunk] = w_ref[lo:lo + chunk, :].T.astype(BF16)
    for lo in range(kr_lo, OFF_BKR, chunk):
        o_ref[:, lo:lo + chunk] = w_ref[lo + B_ROPE:lo + B_ROPE + chunk, :].T.astype(BF16)
    tail = jnp.concatenate([w_ref[kr_lo:kr_hi, :], jnp.zeros((LANES - B_ROPE, tc), F32)], axis=0)
    o_ref[:, OFF_BKR:W_IN_COLS] = tail.T.astype(BF16)


def _prep_w_in(w, tc=256):
    depth, d, n = w.shape
    wt = jnp.swapaxes(w, 1, 2)
    return pl.pallas_call(
        _w_in_repack_kernel,
        grid=(depth, d // tc),
        in_specs=[pl.BlockSpec((None, n, tc), lambda l, i: (l, 0, i))],
        out_specs=pl.BlockSpec((None, tc, W_IN_COLS), lambda l, i: (l, i, 0)),
        out_shape=jax.ShapeDtypeStruct((depth, d, W_IN_COLS), BF16),
        compiler_params=pltpu.CompilerParams(dimension_semantics=("parallel", "parallel"),
                                             vmem_limit_bytes=VMEM_LIMIT_BYTES),
        name="w_in_repack",
    )(wt)


def _prep_w_uq(w):
    depth, r, _ = w.shape
    w = w.reshape(depth, r, B_HEADS, B_NOPE + B_ROPE)
    w = jnp.pad(w, ((0, 0), (0, 0), (0, 0), (0, B_QK_PAD - B_NOPE - B_ROPE)))
    return w.reshape(depth, r, B_HEADS * B_QK_PAD).astype(BF16)


def _prep_w_ukv(w):
    depth, r, _ = w.shape
    w = w.reshape(depth, r, B_HEADS, B_NOPE + B_V)
    wk = w[..., :B_NOPE].reshape(depth, r, B_HEADS * B_NOPE)
    wv = w[..., B_NOPE:].reshape(depth, r, B_HEADS * B_V)
    return wk.astype(BF16), wv.astype(BF16)


def kernel(x, norm_pre, norm_post, w_in, a_q_norm, a_k_norm, b_q_norm, b_kv_norm, b_w_uq, b_w_ukv, c_rpb, w_out):
    batch, seq, d = x.shape
    depth = w_in.shape[0]
    tokens = batch * seq
    assert seq % (2 * C_TQ) == 0 and seq // GRID_W >= C_WROWS
    assert w_in.shape[2] == W_IN_COLS - (LANES - B_ROPE)
    assert seq % INPROJ_ROWS == 0 and tokens % OUTPROJ_ROWS == 0 and seq % ATTN_QUERY_ROWS == 0
    tabs = _rope_tables(seq)
    bias, pat_ids = _nbr_bias(c_rpb, seq)
    w_in_all = _prep_w_in(w_in)
    w_out_all = w_out.astype(BF16)
    wuq, (wuk, wuv) = _prep_w_uq(b_w_uq), _prep_w_ukv(b_w_ukv)
    gpre, gpost, aqn, akn, bqn, bkvn = (
        g[:, None, :] for g in (norm_pre, norm_post, a_q_norm, a_k_norm, b_q_norm, b_kv_norm))
    h = x.reshape(tokens, d)
    for l in range(depth):
        qa, ka, va, ga, qb, kb, vb, gb, qc, kc, vc, gc = _inproj(
            h, seq, l, gpre, w_in_all, wuq, wuk, wuv, aqn, akn, bqn, bkvn, tabs, tm=INPROJ_ROWS)
        mix_a = _dense_attn(qa, ka, va, ga, seq, tq=ATTN_QUERY_ROWS, name="attn_a")
        mix_b = _dense_attn(qb, kb, vb, gb, seq, tq=ATTN_QUERY_ROWS, name="attn_b")
        mix_c = _nbr_attn(qc, kc, vc, bias, pat_ids, l, gc, seq)
        h = _outproj(mix_a, mix_b, mix_c, h, l, w_out_all, gpost, tm=OUTPROJ_ROWS)
    return h.reshape(batch, seq, d)
```

```python
import functools

import jax
import jax.numpy as jnp
import numpy as np
from jax import lax
from jax.experimental import pallas as pl
from jax.experimental.pallas import tpu as pltpu

F32 = jnp.float32
BF16 = jnp.bfloat16
LANES = 128
SUBLANES = 8

GRID_W = 64
HEAD_DIM = 128
A_HEADS = 8
A_KV_HEADS = 2
B_HEADS = 4
B_Q_LORA = 512
B_KV_LORA = 256
B_NOPE = 128
B_ROPE = 64
B_V = 128
B_QK_PAD = 256
C_HEADS = 4
WIN_ROWS = 8
WIN_COLS = 16
ROPE_THETA = 10000.0
NORM_EPS = 1e-6
NEG_BIG = -1e30
LOG2E = 1.4426950408889634

A_Q = A_HEADS * HEAD_DIM
A_KV = A_KV_HEADS * HEAD_DIM
B_G = B_HEADS * B_V
C_W = C_HEADS * HEAD_DIM
D_MIX = A_Q + B_G + C_W

OFF_AQ = 0
OFF_AK = OFF_AQ + A_Q
OFF_AV = OFF_AK + A_KV
OFF_AG = OFF_AV + A_KV
OFF_BCQ = OFF_AG + A_Q
OFF_BCKV = OFF_BCQ + B_Q_LORA
OFF_BG = OFF_BCKV + B_KV_LORA
OFF_CQ = OFF_BG + B_G
OFF_CK = OFF_CQ + C_W
OFF_CV = OFF_CK + C_W
OFF_CG = OFF_CV + C_W
OFF_BKR = OFF_CG + C_W
W_IN_COLS = OFF_BKR + LANES

C_QROWS = 4
C_WROWS = 12
C_TQ = C_QROWS * GRID_W
C_TK = C_WROWS * GRID_W

VMEM_LIMIT_BYTES = 56 * 1024 * 1024

INPROJ_ROWS = 256
ATTN_QUERY_ROWS = 256
OUTPROJ_ROWS = 512
ATTN_HEADS_PER_ITER = 2


def _rms(x, gain):
    return x * lax.rsqrt(jnp.mean(x * x, axis=-1, keepdims=True) + NORM_EPS) * gain


def _rope(x, cos, sin_lo, sin_hi, half):
    up = pltpu.roll(x, LANES - half, 1)
    dn = pltpu.roll(x, half, 1)
    return x * cos + up * sin_lo + dn * sin_hi


def _silu(g):
    return g * (1.0 / (1.0 + jnp.exp(-g)))


_NT = (((1,), (1,)), ((), ()))


def _inproj_kernel(x_ref, gpre_ref, w_ref, wuq_ref, wuk_ref, wuv_ref,
                   aqn_ref, akn_ref, bqn_ref, bkvn_ref,
                   cos_a, slo_a, shi_a, cos_b, slo_b, shi_b,
                   qa_ref, ka_ref, va_ref, ga_ref, qb_ref, kb_ref, vb_ref, gb_ref,
                   qc_ref, kc_ref, vc_ref, gc_ref, h_scr):
    x = x_ref[...]
    h_scr[...] = (x * gpre_ref[...]).astype(BF16)
    inv_rms = lax.rsqrt(jnp.mean(x * x, axis=-1, keepdims=True) + NORM_EPS)
    tm = x.shape[0]

    def proj(off, width):
        return jnp.dot(h_scr[...], w_ref[:, off:off + width], preferred_element_type=F32) * inv_rms

    ones = jnp.ones((tm, HEAD_DIM), BF16)

    ca, sla, sha = cos_a[...], slo_a[...], shi_a[...]
    cb, slb, shb = cos_b[...], slo_b[...], shi_b[...]
    scale_a = HEAD_DIM ** -0.5 * LOG2E
    scale_b = (B_NOPE + B_ROPE) ** -0.5 * LOG2E

    cq = _rms(proj(OFF_BCQ, B_Q_LORA), bqn_ref[...]).astype(BF16)
    ckv = _rms(proj(OFF_BCKV, B_KV_LORA), bkvn_ref[...]).astype(BF16)
    kpe = _rope(proj(OFF_BKR, LANES), cb, slb, shb, 16).astype(BF16)

    pq = proj(OFF_AQ, A_Q)
    for hd in range(A_HEADS):
        sl = slice(hd * HEAD_DIM, (hd + 1) * HEAD_DIM)
        q = _rope(_rms(pq[:, sl], aqn_ref[...]), ca, sla, sha, 32)
        qa_ref[hd] = (q * scale_a).astype(BF16)
    pk = proj(OFF_AK, A_KV)
    pv = proj(OFF_AV, A_KV)
    for hd in range(A_KV_HEADS):
        sl = slice(hd * HEAD_DIM, (hd + 1) * HEAD_DIM)
        ka_ref[hd] = _rope(_rms(pk[:, sl], akn_ref[...]), ca, sla, sha, 32).astype(BF16)
        va_ref[hd, :, 0:HEAD_DIM] = pv[:, sl].astype(BF16)
        va_ref[hd, :, HEAD_DIM:2 * HEAD_DIM] = ones
    ga = _silu(proj(OFF_AG, A_Q)).astype(BF16)
    for hd in range(A_HEADS):
        ga_ref[hd] = ga[:, hd * HEAD_DIM:(hd + 1) * HEAD_DIM]

    qb = jnp.dot(cq, wuq_ref[...], preferred_element_type=F32)
    kn = jnp.dot(ckv, wuk_ref[...], preferred_element_type=F32)
    vb = jnp.dot(ckv, wuv_ref[...], preferred_element_type=F32)
    gb = _silu(proj(OFF_BG, B_G)).astype(BF16)
    for hd in range(B_HEADS):
        lo = hd * B_QK_PAD
        qb_ref[hd, :, 0:B_NOPE] = (qb[:, lo:lo + B_NOPE] * scale_b).astype(BF16)
        qpe = _rope(qb[:, lo + B_NOPE:lo + B_QK_PAD], cb, slb, shb, 16)
        qb_ref[hd, :, B_NOPE:B_QK_PAD] = (qpe * scale_b).astype(BF16)
        kb_ref[hd, :, 0:B_NOPE] = kn[:, hd * B_NOPE:(hd + 1) * B_NOPE].astype(BF16)
        kb_ref[hd, :, B_NOPE:B_QK_PAD] = kpe
        vb_ref[hd, :, 0:B_V] = vb[:, hd * B_V:(hd + 1) * B_V].astype(BF16)
        vb_ref[hd, :, B_V:2 * B_V] = ones
        gb_ref[hd] = gb[:, hd * B_V:(hd + 1) * B_V]

    gc_ref[...] = _silu(proj(OFF_CG, C_W)).astype(BF16)
    qc_ref[...] = (proj(OFF_CQ, C_W) * scale_a).astype(BF16)
    pvc = proj(OFF_CV, C_W)
    for hd in range(C_HEADS):
        vc_ref[:, 2 * hd * HEAD_DIM:(2 * hd + 1) * HEAD_DIM] = (
            pvc[:, hd * HEAD_DIM:(hd + 1) * HEAD_DIM].astype(BF16))
        vc_ref[:, (2 * hd + 1) * HEAD_DIM:(2 * hd + 2) * HEAD_DIM] = ones
    kc_ref[...] = proj(OFF_CK, C_W).astype(BF16)


def _layer_spec(a, layer):
    zeros = (0,) * (a.ndim - 1)
    return pl.BlockSpec((None,) + a.shape[1:], lambda i: (layer,) + zeros,
                        pipeline_mode=pl.Buffered(1))


def _inproj(x2d, seq, layer, gpre, w_all, wuq, wuk, wuv, aqn, akn, bqn, bkvn, tabs, tm):
    t, d = x2d.shape
    n_s = seq // tm
    row = lambda i: (i, 0)
    tab = lambda i: (i % n_s, 0)
    resident = functools.partial(_layer_spec, layer=layer)

    head_major = ((A_HEADS, HEAD_DIM), (A_KV_HEADS, HEAD_DIM), (A_KV_HEADS, 2 * HEAD_DIM),
                  (A_HEADS, HEAD_DIM), (B_HEADS, B_QK_PAD), (B_HEADS, B_QK_PAD),
                  (B_HEADS, 2 * B_V), (B_HEADS, B_V))
    token_major = (C_W, C_W, 2 * C_W, C_W)
    out_specs = ([pl.BlockSpec((nh, tm, n), lambda i: (0, i, 0)) for nh, n in head_major]
                 + [pl.BlockSpec((tm, n), row) for n in token_major])
    out_shape = ([jax.ShapeDtypeStruct((nh, t, n), BF16) for nh, n in head_major]
                 + [jax.ShapeDtypeStruct((t, n), BF16) for n in token_major])
    return pl.pallas_call(
        _inproj_kernel,
        grid=(t // tm,),
        in_specs=[pl.BlockSpec((tm, d), row), resident(gpre), resident(w_all), resident(wuq),
                  resident(wuk), resident(wuv), resident(aqn), resident(akn), resident(bqn),
                  resident(bkvn)] + [pl.BlockSpec((tm, LANES), tab)] * 6,
        out_specs=out_specs,
        out_shape=out_shape,
        scratch_shapes=[pltpu.VMEM((tm, d), BF16)],
        compiler_params=pltpu.CompilerParams(dimension_semantics=("parallel",),
                                             vmem_limit_bytes=VMEM_LIMIT_BYTES),
        name="inproj",
    )(x2d, gpre, w_all, wuq, wuk, wuv, aqn, akn, bqn, bkvn, *tabs)


def _softmax_pv(s, v_ext, dv):
    m = jnp.max(s, axis=-1, keepdims=True)
    p = jnp.exp2(s - m).astype(BF16)
    oe = jnp.dot(p, v_ext, preferred_element_type=F32)
    return oe[:, :dv] * (1.0 / oe[:, dv:])


def _dense_attn_kernel(q_ref, k_ref, v_ref, g_ref, o_ref, s_a, s_b, *, dv, tq, heads_per_iter):
    heads, seq, _ = q_ref.shape
    group = heads // k_ref.shape[0]
    n_t = seq // tq
    bufs = (s_a, s_b)

    def qk(hd, u, s_ref):
        s_ref[...] = lax.dot_general(q_ref[hd, u * tq:(u + 1) * tq, :], k_ref[hd // group], _NT,
                                     preferred_element_type=F32)

    def finish(hd, u, s_ref):
        o = _softmax_pv(s_ref[...], v_ref[hd // group], dv)
        rows = slice(u * tq, (u + 1) * tq)
        o_ref[hd, rows, :] = (o * g_ref[hd, rows, :].astype(F32)).astype(BF16)

    def body(it, carry):
        for h in range(heads_per_iter):
            hd = it * heads_per_iter + h
            nxt = jnp.minimum(hd + 1, heads - 1)
            for u in range(n_t):
                if u + 1 < n_t:
                    qk(hd, u + 1, bufs[(u + 1) % 2])
                else:
                    qk(nxt, 0, bufs[0])
                finish(hd, u, bufs[u % 2])
        return carry

    qk(0, 0, s_a)
    lax.fori_loop(0, heads // heads_per_iter, body, 0)


def _dense_attn(q, k, v_ext, gate, seq, *, tq, name):
    heads, t, dq = q.shape
    kv_heads = k.shape[0]
    dv = gate.shape[2]
    assert (seq // tq) % 2 == 0 and v_ext.shape[2] == 2 * dv
    assert heads % kv_heads == 0 and heads % ATTN_HEADS_PER_ITER == 0
    blk = lambda b: (0, b, 0)
    kern = functools.partial(_dense_attn_kernel, dv=dv, tq=tq, heads_per_iter=ATTN_HEADS_PER_ITER)
    return pl.pallas_call(
        kern,
        grid=(t // seq,),
        in_specs=[pl.BlockSpec((heads, seq, dq), blk),
                  pl.BlockSpec((kv_heads, seq, dq), blk),
                  pl.BlockSpec((kv_heads, seq, 2 * dv), blk),
                  pl.BlockSpec((heads, seq, dv), blk)],
        out_specs=pl.BlockSpec((heads, seq, dv), blk),
        out_shape=jax.ShapeDtypeStruct((heads, t, dv), BF16),
        scratch_shapes=[pltpu.VMEM((tq, seq), F32), pltpu.VMEM((tq, seq), F32)],
        compiler_params=pltpu.CompilerParams(dimension_semantics=("parallel",),
                                             vmem_limit_bytes=VMEM_LIMIT_BYTES),
        name=name,
    )(q, k, v_ext, gate)


def _nbr_attn_kernel(pid_ref, q_ref, qn_ref, k_ref, kn_ref, v_ref, bias_ref, g_ref,
                     o_ref, s_a, s_b, *, max_ws, nb):
    j = pl.program_id(0)
    i0 = (2 * j) % nb

    def window_start(i):
        ws = jnp.clip(C_QROWS * i - WIN_ROWS // 2, 0, max_ws)
        return pl.multiple_of(ws * GRID_W, GRID_W)

    def qk(q, keys_ref, i, s_ref, hd):
        sl = slice(hd * HEAD_DIM, (hd + 1) * HEAD_DIM)
        s_ref[hd] = lax.dot_general(q[:, sl], keys_ref[pl.ds(window_start(i), C_TK), sl], _NT,
                                    preferred_element_type=F32)

    def finish(s_ref, i, rows, hd):
        sl = slice(hd * HEAD_DIM, (hd + 1) * HEAD_DIM)
        vw = v_ref[pl.ds(window_start(i), C_TK), 2 * hd * HEAD_DIM:2 * (hd + 1) * HEAD_DIM]
        s = s_ref[hd] + bias_ref[pid_ref[i], hd]
        o = _softmax_pv(s, vw, HEAD_DIM) * g_ref[rows, sl].astype(F32)
        o_ref[rows, sl] = o.astype(BF16)

    first, second = slice(0, C_TQ), slice(C_TQ, 2 * C_TQ)

    @pl.when(j == 0)
    def _():
        for hd in range(C_HEADS):
            qk(q_ref[first, :], k_ref, i0, s_a, hd)

    for hd in range(C_HEADS):
        qk(q_ref[second, :], k_ref, i0 + 1, s_b, hd)
        finish(s_a, i0, first, hd)
    for hd in range(C_HEADS):
        qk(qn_ref[...], kn_ref, (i0 + 2) % nb, s_a, hd)
        finish(s_b, i0 + 1, second, hd)


def _nbr_attn(q, k, v, bias, pat_ids, layer, gate, seq):
    t = q.shape[0]
    nb = seq // C_TQ
    n_blocks = t // C_TQ
    assert nb % 2 == 0
    kern = functools.partial(_nbr_attn_kernel, max_ws=seq // GRID_W - C_WROWS, nb=nb)
    nxt = lambda j: jnp.minimum(2 * j + 2, n_blocks - 1)
    n_pat = bias.shape[0]
    bias_spec = pl.BlockSpec((n_pat, None, C_HEADS, C_TQ, C_TK),
                             lambda j, pid: (0, layer, 0, 0, 0), pipeline_mode=pl.Buffered(1))
    grid_spec = pltpu.PrefetchScalarGridSpec(
        num_scalar_prefetch=1,
        grid=(n_blocks // 2,),
        in_specs=[pl.BlockSpec((2 * C_TQ, C_W), lambda j, pid: (j, 0)),
                  pl.BlockSpec((C_TQ, C_W), lambda j, pid: (nxt(j), 0)),
                  pl.BlockSpec((seq, C_W), lambda j, pid: (2 * j // nb, 0)),
                  pl.BlockSpec((seq, C_W), lambda j, pid: (nxt(j) // nb, 0)),
                  pl.BlockSpec((seq, 2 * C_W), lambda j, pid: (2 * j // nb, 0)),
                  bias_spec,
                  pl.BlockSpec((2 * C_TQ, C_W), lambda j, pid: (j, 0))],
        out_specs=pl.BlockSpec((2 * C_TQ, C_W), lambda j, pid: (j, 0)),
        scratch_shapes=[pltpu.VMEM((C_HEADS, C_TQ, C_TK), F32)] * 2)
    return pl.pallas_call(
        kern,
        grid_spec=grid_spec,
        out_shape=jax.ShapeDtypeStruct((t, C_W), BF16),
        compiler_params=pltpu.CompilerParams(dimension_semantics=("arbitrary",),
                                             vmem_limit_bytes=VMEM_LIMIT_BYTES),
        name="attn_c",
    )(pat_ids, q, q, k, k, v, bias, gate)


def _nbr_bias(rpb, seq):
    depth, heads, n_dr, n_dc = rpb.shape
    g = depth * heads
    rows = seq // GRID_W
    nb = seq // C_TQ
    neg = NEG_BIG * LOG2E
    lead = GRID_W - WIN_COLS
    u = jnp.pad(rpb.reshape(g, n_dr, n_dc) * LOG2E,
                ((0, 0), (0, 1), (lead, 2 * GRID_W - n_dc - lead)))
    blk = np.arange(nb)[:, None, None]
    r = C_QROWS * blk + np.arange(C_QROWS)[None, :, None]
    ws = np.clip(C_QROWS * blk - WIN_ROWS // 2, 0, rows - C_WROWS)
    kr = ws + np.arange(C_WROWS)[None, None, :]
    r0 = np.clip(r - WIN_ROWS // 2, 0, rows - WIN_ROWS)
    valid_r = (kr >= r0) & (kr < r0 + WIN_ROWS)
    slab_id = np.where(valid_r, kr - r + (WIN_ROWS - 1), -1)
    pats, pat_ids = np.unique(slab_id.reshape(nb, -1), axis=0, return_inverse=True)
    pats = np.where(pats < 0, n_dr, pats).reshape(-1, C_QROWS, C_WROWS)
    n_pat = pats.shape[0]
    table = tuple(tuple(tuple(int(s) for s in row) for row in pat) for pat in pats)
    bias = pl.pallas_call(
        functools.partial(_nbr_bias_kernel, table=table, n_dr=n_dr, neg=neg),
        grid=(g,),
        in_specs=[pl.BlockSpec((None, n_dr + 1, 2 * GRID_W), lambda i: (i, 0, 0))],
        out_specs=pl.BlockSpec((n_pat, None, C_TQ, C_TK), lambda i: (0, i, 0, 0)),
        out_shape=jax.ShapeDtypeStruct((n_pat, g, C_TQ, C_TK), F32),
        scratch_shapes=[pltpu.VMEM((n_dr + 1, GRID_W, 2 * GRID_W), F32)],
        compiler_params=pltpu.CompilerParams(dimension_semantics=("parallel",)),
        name="nbr_bias",
    )(u)
    return (bias.reshape(n_pat, depth, heads, C_TQ, C_TK),
            jnp.asarray(pat_ids.reshape(nb), jnp.int32))


def _nbr_bias_kernel(u_ref, o_ref, slab_scr, *, table, n_dr, neg):
    shape = (GRID_W, 2 * GRID_W)
    lane = lax.broadcasted_iota(jnp.int32, shape, 1)
    qc = lax.broadcasted_iota(jnp.int32, shape, 0)
    kc = lane % GRID_W
    left = lane < GRID_W
    c0 = jnp.clip(qc - WIN_COLS // 2, 0, GRID_W - WIN_COLS)
    valid = (kc >= c0) & (kc < c0 + WIN_COLS)
    for dr in range(n_dr):
        rows = jnp.broadcast_to(u_ref[dr:dr + 1, :], shape)
        skew = pltpu.roll(rows, GRID_W + 1, 1, stride=1, stride_axis=0)
        both = jnp.where(left, skew, pltpu.roll(skew, GRID_W, 1))
        slab_scr[dr] = jnp.where(valid, both, neg)
    slab_scr[n_dr] = jnp.full(shape, neg, F32)
    for p, pat in enumerate(table):
        for q, row in enumerate(pat):
            for kp in range(len(row) // 2):
                tile = jnp.where(left, slab_scr[row[2 * kp]], slab_scr[row[2 * kp + 1]])
                o_ref[p, q * GRID_W:(q + 1) * GRID_W, 2 * kp * GRID_W:2 * (kp + 1) * GRID_W] = tile


def _outproj_kernel(ma_ref, mb_ref, mc_ref, x_ref, w_ref, gpost_ref, o_ref, y_a, y_b, *, n_tiles):
    i = pl.program_id(0)
    tm, d = o_ref.shape
    n_chunks = 8
    cols = d // n_chunks
    body_rows = (tm - SUBLANES) // (n_chunks - 1)
    row_starts = [c * body_rows for c in range(n_chunks)] + [tm]
    assert body_rows % SUBLANES == 0 and row_starts[-2] == tm - SUBLANES

    def dots(y_ref, c, after):
        heads_a = [ma_ref[hd] for hd in range(A_HEADS)]
        if after is not None:
            zero = pltpu.bitcast(after, F32)
            packed = 2 * SUBLANES
            top = heads_a[0][0:packed, :].astype(F32) + jnp.concatenate([zero, zero], axis=0)
            heads_a[0] = jnp.concatenate([top.astype(BF16), heads_a[0][packed:, :]], axis=0)
        ma = jnp.concatenate(heads_a, axis=1)
        mb = jnp.concatenate([mb_ref[hd] for hd in range(B_HEADS)], axis=1)
        cs = slice(c * cols, (c + 1) * cols)
        y = jnp.dot(ma, w_ref[0:A_Q, cs], preferred_element_type=F32)
        y += jnp.dot(mb, w_ref[A_Q:A_Q + B_G, cs], preferred_element_type=F32)
        y += jnp.dot(mc_ref[...], w_ref[A_Q + B_G:D_MIX, cs], preferred_element_type=F32)
        y_ref[:, cs] = y

    def epilogue(y_ref, c):
        rs = slice(row_starts[c], row_starts[c + 1])
        rows = row_starts[c + 1] - row_starts[c]
        out = x_ref[rs, :] + _rms(y_ref[rs, :], gpost_ref[...])
        o_ref[rs, :] = out
        bits = pltpu.bitcast(out, jnp.uint32)
        acc = None
        for r in range(0, rows, SUBLANES):
            for l in range(0, d, LANES):
                word = bits[r:r + SUBLANES, l:l + LANES]
                acc = word if acc is None else acc | word
        return (acc >> 16) >> 16

    def step(y_cur, y_prev):
        after = None
        for c in range(n_chunks):
            if y_cur is not None:
                dots(y_cur, c, after)
            if y_prev is not None:
                after = epilogue(y_prev, c)

    inner = (i > 0) & (i < n_tiles)

    @pl.when(i == 0)
    def _():
        step(y_a, None)

    @pl.when(inner & (i % 2 == 1))
    def _():
        step(y_b, y_a)

    @pl.when(inner & (i % 2 == 0))
    def _():
        step(y_a, y_b)

    @pl.when(i == n_tiles)
    def _():
        step(None, y_a if (n_tiles - 1) % 2 == 0 else y_b)


def _outproj(ma, mb, mc, x2d, layer, w_all, gpost_all, tm):
    t, d = x2d.shape
    n_tiles = t // tm
    cur = lambda i: (jnp.minimum(i, n_tiles - 1), 0)
    cur_hm = lambda i: (0, jnp.minimum(i, n_tiles - 1), 0)
    prev = lambda i: (jnp.maximum(i - 1, 0), 0)
    return pl.pallas_call(
        functools.partial(_outproj_kernel, n_tiles=n_tiles),
        grid=(n_tiles + 1,),
        in_specs=[pl.BlockSpec((A_HEADS, tm, HEAD_DIM), cur_hm),
                  pl.BlockSpec((B_HEADS, tm, B_V), cur_hm),
                  pl.BlockSpec((tm, C_W), cur), pl.BlockSpec((tm, d), prev),
                  _layer_spec(w_all, layer), _layer_spec(gpost_all, layer)],
        out_specs=pl.BlockSpec((tm, d), prev),
        out_shape=jax.ShapeDtypeStruct((t, d), F32),
        scratch_shapes=[pltpu.VMEM((tm, d), F32), pltpu.VMEM((tm, d), F32)],
        compiler_params=pltpu.CompilerParams(dimension_semantics=("arbitrary",),
                                             vmem_limit_bytes=VMEM_LIMIT_BYTES),
        name="outproj",
    )(ma, mb, mc, x2d, w_all, gpost_all)


def _rope_tables(seq):
    t = np.arange(seq)
    pos = np.stack([t // GRID_W, t % GRID_W], axis=0).astype(np.float64)

    def tables(n, lanes_used):
        j = np.arange(LANES)
        axis = np.minimum(j // n, 1)
        i = j % (n // 2)
        inv_freq = 1.0 / (ROPE_THETA ** (np.arange(0, n, 2, dtype=np.float64) / n))
        ang = pos[axis, :].T * inv_freq[i][None, :]
        used = (j < lanes_used)[None, :]
        low = ((j % n) < n // 2)[None, :]
        cos = np.where(used, np.cos(ang), 0.0)
        sin = np.where(used, np.sin(ang), 0.0)
        return tuple(jnp.asarray(a, F32) for a in (cos, np.where(low, -sin, 0.0), np.where(low, 0.0, sin)))

    return tables(HEAD_DIM // 2, LANES) + tables(B_ROPE // 2, B_ROPE)


def _w_in_repack_kernel(w_ref, o_ref):
    kr_lo = OFF_BG
    kr_hi = kr_lo + B_ROPE
    tc = w_ref.shape[1]
    chunk = 256
    for lo in range(0, kr_lo, chunk):
        o_ref[:, lo:lo + chunk] = w_ref[lo:lo + chunk, :].T.astype(BF16)
    for lo in range(kr_lo, OFF_BKR, chunk):
        o_ref[:, lo:lo + chunk] = w_ref[lo + B_ROPE:lo + B_ROPE + chunk, :].T.astype(BF16)
    tail = jnp.concatenate([w_ref[kr_lo:kr_hi, :], jnp.zeros((LANES - B_ROPE, tc), F32)], axis=0)
    o_ref[:, OFF_BKR:W_IN_COLS] = tail.T.astype(BF16)


def _prep_w_in(w, tc=256):
    depth, d, n = w.shape
    wt = jnp.swapaxes(w, 1, 2)
    return pl.pallas_call(
        _w_in_repack_kernel,
        grid=(depth, d // tc),
        in_specs=[pl.BlockSpec((None, n, tc), lambda l, i: (l, 0, i))],
        out_specs=pl.BlockSpec((None, tc, W_IN_COLS), lambda l, i: (l, i, 0)),
        out_shape=jax.ShapeDtypeStruct((depth, d, W_IN_COLS), BF16),
        compiler_params=pltpu.CompilerParams(dimension_semantics=("parallel", "parallel"),
                                             vmem_limit_bytes=VMEM_LIMIT_BYTES),
        name="w_in_repack",
    )(wt)


def _prep_w_uq(w):
    depth, r, _ = w.shape
    w = w.reshape(depth, r, B_HEADS, B_NOPE + B_ROPE)
    w = jnp.pad(w, ((0, 0), (0, 0), (0, 0), (0, B_QK_PAD - B_NOPE - B_ROPE)))
    return w.reshape(depth, r, B_HEADS * B_QK_PAD).astype(BF16)


def _prep_w_ukv(w):
    depth, r, _ = w.shape
    w = w.reshape(depth, r, B_HEADS, B_NOPE + B_V)
    wk = w[..., :B_NOPE].reshape(depth, r, B_HEADS * B_NOPE)
    wv = w[..., B_NOPE:].reshape(depth, r, B_HEADS * B_V)
    return wk.astype(BF16), wv.astype(BF16)


def kernel(x, norm_pre, norm_post, w_in, a_q_norm, a_k_norm, b_q_norm, b_kv_norm, b_w_uq, b_w_ukv, c_rpb, w_out):
    batch, seq, d = x.shape
    depth = w_in.shape[0]
    tokens = batch * seq
    assert seq % (2 * C_TQ) == 0 and seq // GRID_W >= C_WROWS
    assert w_in.shape[2] == W_IN_COLS - (LANES - B_ROPE)
    assert seq % INPROJ_ROWS == 0 and tokens % OUTPROJ_ROWS == 0 and seq % ATTN_QUERY_ROWS == 0
    tabs = _rope_tables(seq)
    bias, pat_ids = _nbr_bias(c_rpb, seq)
    w_in_all = _prep_w_in(w_in)
    w_out_all = w_out.astype(BF16)
    wuq, (wuk, wuv) = _prep_w_uq(b_w_uq), _prep_w_ukv(b_w_ukv)
    gpre, gpost, aqn, akn, bqn, bkvn = (
        g[:, None, :] for g in (norm_pre, norm_post, a_q_norm, a_k_norm, b_q_norm, b_kv_norm))
    h = x.reshape(tokens, d)
    for l in range(depth):
        qa, ka, va, ga, qb, kb, vb, gb, qc, kc, vc, gc = _inproj(
            h, seq, l, gpre, w_in_all, wuq, wuk, wuv, aqn, akn, bqn, bkvn, tabs, tm=INPROJ_ROWS)
        mix_a = _dense_attn(qa, ka, va, ga, seq, tq=ATTN_QUERY_ROWS, name="attn_a")
        mix_b = _dense_attn(qb, kb, vb, gb, seq, tq=ATTN_QUERY_ROWS, name="attn_b")
        mix_c = _nbr_attn(qc, kc, vc, bias, pat_ids, l, gc, seq)
        h = _outproj(mix_a, mix_b, mix_c, h, l, w_out_all, gpost, tm=OUTPROJ_ROWS)
    return h.reshape(batch, seq, d)
```
